```python
import math
import jax, jax.numpy as jnp
from jax import lax
import numpy as np

D_MODEL = 2048
BATCH = 16
SEQ = 2048
DEPTH = 2
DEC_BATCH = 2
DEC_SEQ = 16384
PAST_LEN = 128

ATT_GROUPS = ((128, 1), (512, 4), (2048, 16))
ATT_HEADS = 4
ATT_HEAD_DIM = 128
ROPE_DIM = ATT_HEAD_DIM // 4
ROPE_THETA = 500000.0
ML_HEADS = 4
ML_QK_DIM = 128
ML_V_DIM = 256
ML_CHUNK = 128
ML_CONV = 5
N_EXPERTS = 16
EXPERT_FF = 2048
CAPACITY_FACTOR = 2
EPS = 1e-6
NEG_INF = -1e30

ATT_W = len(ATT_GROUPS) * ATT_HEADS * ATT_HEAD_DIM
ATT_OUT_W = ATT_HEADS * ATT_HEAD_DIM
ML_QK_W = ML_HEADS * ML_QK_DIM
ML_V_W = ML_HEADS * ML_V_DIM
ML_GATE_W = 4 * ML_HEADS
IN_SIZES = (ATT_W, ATT_W, ATT_W, ML_QK_W, ML_QK_W, ML_V_W, ML_V_W, ML_GATE_W, 2 * D_MODEL)
IN_W = 3 * ATT_W + 2 * ML_QK_W + 2 * ML_V_W + ML_GATE_W + 2 * D_MODEL

kernel_name = "hybrid_mlstm_dilated_attn_ec_moe_encoder"


def rms_norm(x, w):
    x32 = x.astype(jnp.float32)
    y = x32 * lax.rsqrt(jnp.mean(x32 * x32, axis=-1, keepdims=True) + EPS)
    return (y * w.astype(jnp.float32)).astype(x.dtype)


def partial_rope(x, positions):
    half = ROPE_DIM // 2
    freqs = jnp.float32(ROPE_THETA) ** (-jnp.arange(half, dtype=jnp.float32) / half)
    ang = positions.astype(jnp.float32)[:, None] * freqs[None, :]
    cos = jnp.cos(ang)[None, :, None, :]
    sin = jnp.sin(ang)[None, :, None, :]
    x32 = x.astype(jnp.float32)
    x1 = x32[..., :half]
    x2 = x32[..., half:ROPE_DIM]
    out = jnp.concatenate([x1 * cos - x2 * sin, x2 * cos + x1 * sin, x32[..., ROPE_DIM:]], axis=-1)
    return out.astype(x.dtype)


def dilated_window_attention(q, k, v, dilation, half):
    B, T, H, dh = q.shape
    S = T // dilation
    nb = -(-S // half)
    s_pad = nb * half

    def to_sub(x):
        x = x.reshape(B, S, dilation, H, dh).transpose(0, 2, 3, 1, 4)
        return jnp.pad(x, ((0, 0), (0, 0), (0, 0), (0, s_pad - S), (0, 0)))

    def key_blocks(x):
        x = jnp.pad(to_sub(x), ((0, 0), (0, 0), (0, 0), (half, half), (0, 0)))
        x = x.reshape(B, dilation, H, nb + 2, half, dh)
        return jnp.concatenate([x[:, :, :, :-2], x[:, :, :, 1:-1], x[:, :, :, 2:]], axis=4)

    qs = to_sub(q).reshape(B, dilation, H, nb, half, dh)
    ks = key_blocks(k)
    vs = key_blocks(v)
    q_pos = jnp.arange(nb)[:, None] * half + jnp.arange(half)[None, :]
    k_pos = (jnp.arange(nb)[:, None] - 1) * half + jnp.arange(3 * half)[None, :]
    dist = q_pos[:, :, None] - k_pos[:, None, :]
    valid = (jnp.abs(dist) <= half) & (k_pos[:, None, :] >= 0) & (k_pos[:, None, :] < S)
    scores = jnp.einsum('brhnqd,brhnkd->brhnqk', qs, ks)
    scores = jnp.where(valid, scores, NEG_INF)
    lse = jax.nn.logsumexp(scores, axis=-1)
    p = jnp.exp(scores - lse[..., None])
    o = jnp.einsum('brhnqk,brhnkd->brhnqd', p, vs)
    o = o.reshape(B, dilation, H, s_pad, dh)[:, :, :, :S]
    o = o.transpose(0, 3, 1, 2, 4).reshape(B, T, H, dh)
    lse = lse.reshape(B, dilation, H, s_pad)[:, :, :, :S]
    lse = lse.transpose(0, 3, 1, 2).reshape(B, T, H)
    return o, lse


def attention_branch(att_q, att_k, att_v):
    B, T, _ = att_q.shape
    G = len(ATT_GROUPS)
    pos = jnp.arange(T)
    q = partial_rope(att_q.reshape(B, T, G * ATT_HEADS, ATT_HEAD_DIM), pos)
    k = partial_rope(att_k.reshape(B, T, G * ATT_HEADS, ATT_HEAD_DIM), pos)
    q = q.astype(jnp.float32).reshape(B, T, G, ATT_HEADS, ATT_HEAD_DIM) * (ATT_HEAD_DIM ** -0.5)
    k = k.astype(jnp.float32).reshape(B, T, G, ATT_HEADS, ATT_HEAD_DIM)
    v = att_v.astype(jnp.float32).reshape(B, T, G, ATT_HEADS, ATT_HEAD_DIM)
    outs, lses = [], []
    for g, (window, dil) in enumerate(ATT_GROUPS):
        half = window // (2 * dil)
        o_g, lse_g = dilated_window_attention(q[:, :, g], k[:, :, g], v[:, :, g], dil, half)
        outs.append(o_g)
        lses.append(lse_g)
    w = jax.nn.softmax(jnp.stack(lses, axis=0), axis=0)
    o = jnp.sum(w[..., None] * jnp.stack(outs, axis=0), axis=0)
    return o.reshape(B, T, ATT_OUT_W).astype(att_q.dtype)


def mlstm_scan(q, k, v, ig, lf):
    B, T, H, dk = q.shape
    dv = v.shape[-1]
    L = ML_CHUNK
    nc = T // L
    qc = q.reshape(B, nc, L, H, dk).transpose(1, 0, 3, 2, 4)
    kc = k.reshape(B, nc, L, H, dk).transpose(1, 0, 3, 2, 4)
    vc = v.reshape(B, nc, L, H, dv).transpose(1, 0, 3, 2, 4)
    igc = ig.reshape(B, nc, L, H).transpose(1, 0, 3, 2)
    lfc = lf.reshape(B, nc, L, H).transpose(1, 0, 3, 2)
    lower = jnp.tril(jnp.ones((L, L), dtype=bool))

    def step(carry, inp):
        C, n, m = carry
        qb, kb, vb, ib, fb = inp
        b = jnp.cumsum(fb, axis=-1)
        dlog = jnp.where(lower, b[..., :, None] - b[..., None, :] + ib[..., None, :], NEG_INF)
        inter = b + m[..., None]
        m_i = jnp.maximum(inter, jnp.max(dlog, axis=-1))
        w = jnp.exp(dlog - m_i[..., None])
        s = jnp.einsum('bhid,bhjd->bhij', qb, kb) * w
        sc = jnp.exp(inter - m_i)
        num = jnp.einsum('bhij,bhje->bhie', s, vb) + sc[..., None] * jnp.einsum('bhid,bhde->bhie', qb, C)
        den = jnp.sum(s, axis=-1) + sc * jnp.einsum('bhid,bhd->bhi', qb, n)
        h = num / jnp.maximum(jnp.abs(den), jnp.exp(-m_i))[..., None]
        b_last = b[..., -1]
        g = b_last[..., None] - b + ib
        m_new = jnp.maximum(b_last + m, jnp.max(g, axis=-1))
        decay = jnp.exp(b_last + m - m_new)
        wg = jnp.exp(g - m_new[..., None])
        C_new = decay[..., None, None] * C + jnp.einsum('bhj,bhjd,bhje->bhde', wg, kb, vb)
        n_new = decay[..., None] * n + jnp.einsum('bhj,bhjd->bhd', wg, kb)
        return (C_new, n_new, m_new), h

    init = (jnp.zeros((B, H, dk, dv), jnp.float32), jnp.zeros((B, H, dk), jnp.float32),
            jnp.zeros((B, H), jnp.float32))
    _, h = lax.scan(step, init, (qc, kc, vc, igc, lfc))
    return h.transpose(1, 0, 3, 2, 4).reshape(B, T, H, dv)


def centred_dwconv(x, w, b):
    K, C = w.shape
    out = lax.conv_general_dilated(x, w[:, None, :], window_strides=(1,), padding=[(K // 2, K // 2)],
                                   dimension_numbers=('NWC', 'WIO', 'NWC'), feature_group_count=C)
    return out + b


def mlstm_branch(ml_q, ml_k, ml_v, ml_o, ml_gates, conv_w, conv_b, b_mgate, ml_norm_w):
    B, T, _ = ml_q.shape
    qk = jax.nn.silu(centred_dwconv(jnp.concatenate([ml_q, ml_k], axis=-1), conv_w, conv_b))
    q = qk[..., :ML_QK_W].astype(jnp.float32).reshape(B, T, ML_HEADS, ML_QK_DIM) * (ML_QK_DIM ** -0.5)
    k = qk[..., ML_QK_W:].astype(jnp.float32).reshape(B, T, ML_HEADS, ML_QK_DIM)
    v = ml_v.astype(jnp.float32).reshape(B, T, ML_HEADS, ML_V_DIM)
    gates = (ml_gates + b_mgate).astype(jnp.float32).reshape(B, T, 4, ML_HEADS)
    ig_f, lf_f = gates[:, :, 0], jax.nn.log_sigmoid(gates[:, :, 1])
    ig_b, lf_b = gates[:, :, 2], jax.nn.log_sigmoid(gates[:, :, 3])
    h_f = mlstm_scan(q, k, v, ig_f, lf_f)
    flip = lambda a: jnp.flip(a, axis=1)
    h_b = flip(mlstm_scan(flip(q), flip(k), flip(v), flip(ig_b), flip(lf_b)))
    h = h_f + h_b
    h = h * lax.rsqrt(jnp.mean(h * h, axis=-1, keepdims=True) + EPS)
    h = h.reshape(B, T, ML_V_W) * ml_norm_w.astype(jnp.float32)
    return (jax.nn.sigmoid(ml_o.astype(jnp.float32)) * h).astype(ml_q.dtype)


def mixer_sublayer(h, w_in, b_mgate, conv_w, conv_b, ml_norm_w, w_ml_out, w_att_out, w_o):
    proj = jnp.einsum('btd,dn->btn', h, w_in)
    points = np.cumsum(IN_SIZES)[:-1].tolist()
    att_q, att_k, att_v, ml_q, ml_k, ml_v, ml_o, ml_gates, merge = jnp.split(proj, points, axis=-1)
    y_att = jnp.einsum('btc,cd->btd', attention_branch(att_q, att_k, att_v), w_att_out)
    y_ml = jnp.einsum('btc,cd->btd',
                      mlstm_branch(ml_q, ml_k, ml_v, ml_o, ml_gates, conv_w, conv_b, b_mgate, ml_norm_w),
                      w_ml_out)
    g_ml, g_att = jnp.split(merge, 2, axis=-1)
    merged = jax.nn.sigmoid(g_ml) * y_ml + jax.nn.sigmoid(g_att) * y_att
    return jnp.einsum('btd,de->bte', merged, w_o)


def moe_sublayer(h, w_router, w_exp_gate, w_exp_up, w_exp_down):
    B, T, D = h.shape
    N = B * T
    cap = CAPACITY_FACTOR * N // N_EXPERTS
    tok = h.reshape(N, D)
    aff = jax.nn.softmax(jnp.einsum('nd,de->ne', tok.astype(jnp.float32), w_router.astype(jnp.float32)), axis=-1)
    gate_vals, idx = lax.top_k(aff.T, cap)
    xs = tok[idx]
    hid = jax.nn.silu(jnp.einsum('ecd,edf->ecf', xs, w_exp_gate)) * jnp.einsum('ecd,edf->ecf', xs, w_exp_up)
    out = jnp.einsum('ecf,efd->ecd', hid, w_exp_down) * gate_vals[..., None].astype(h.dtype)
    y = jnp.zeros_like(tok).at[idx.reshape(-1)].add(out.reshape(-1, D))
    return y.reshape(B, T, D)


def run_trunk(x, c, norm1_w, norm2_w, w_ada, b_ada, w_in, b_mgate, conv_w, conv_b, ml_norm_w,
              w_ml_out, w_att_out, w_o, w_router, w_exp_gate, w_exp_up, w_exp_down, final_norm_w):
    for l in range(DEPTH):
        mod = jnp.einsum('bd,dm->bm', jax.nn.silu(c), w_ada[l]) + b_ada[l]
        shift1, scale1, gate1, shift2, scale2, gate2 = [m[:, None, :] for m in jnp.split(mod, 6, axis=-1)]
        h = rms_norm(x, norm1_w[l]) * (1 + scale1) + shift1
        x = x + gate1 * mixer_sublayer(h, w_in[l], b_mgate[l], conv_w[l], conv_b[l], ml_norm_w[l],
                                       w_ml_out[l], w_att_out[l], w_o[l])
        h = rms_norm(x, norm2_w[l]) * (1 + scale2) + shift2
        x = x + gate2 * moe_sublayer(h, w_router[l], w_exp_gate[l], w_exp_up[l], w_exp_down[l])
    return rms_norm(x, final_norm_w)


def setup_inputs(seed: int = 0) -> dict:
    key = jax.random.key(seed)
    ks = jax.random.split(key, 32)
    D = D_MODEL
    nrm = lambda k, shape, s: jax.random.normal(k, shape, jnp.float32) * s
    b_mgate = jnp.concatenate([
        nrm(ks[10], (DEPTH, ML_HEADS), 0.1),
        3.0 + 3.0 * jax.random.uniform(ks[11], (DEPTH, ML_HEADS), jnp.float32),
        nrm(ks[12], (DEPTH, ML_HEADS), 0.1),
        3.0 + 3.0 * jax.random.uniform(ks[13], (DEPTH, ML_HEADS), jnp.float32)], axis=-1)
    return {
        "x_prompt": nrm(ks[0], (BATCH, SEQ, D), 1.0),
        "x_sample": nrm(ks[1], (DEC_BATCH, DEC_SEQ, D), 1.0),
        "c_prompt": nrm(ks[2], (BATCH, D), 1.0),
        "c_sample": nrm(ks[3], (DEC_BATCH, D), 1.0),
        "norm1_w": 1.0 + nrm(ks[4], (DEPTH, D), 0.05),
        "norm2_w": 1.0 + nrm(ks[5], (DEPTH, D), 0.05),
        "w_ada": nrm(ks[6], (DEPTH, D, 6 * D), 0.5 * D ** -0.5),
        "b_ada": nrm(ks[7], (DEPTH, 6 * D), 0.02),
        "w_in": nrm(ks[8], (DEPTH, D, IN_W), D ** -0.5),
        "b_mgate": b_mgate,
        "conv_w": nrm(ks[14], (DEPTH, ML_CONV, 2 * ML_QK_W), ML_CONV ** -0.5),
        "conv_b": nrm(ks[15], (DEPTH, 2 * ML_QK_W), 0.02),
        "ml_norm_w": 1.0 + nrm(ks[16], (DEPTH, ML_V_W), 0.05),
        "w_ml_out": nrm(ks[17], (DEPTH, ML_V_W, D), ML_V_W ** -0.5),
        "w_att_out": nrm(ks[18], (DEPTH, ATT_OUT_W, D), ATT_OUT_W ** -0.5),
        "w_o": nrm(ks[19], (DEPTH, D, D), D ** -0.5),
        "w_router": nrm(ks[20], (DEPTH, D, N_EXPERTS), D ** -0.5),
        "w_exp_gate": nrm(ks[21], (DEPTH, N_EXPERTS, D, EXPERT_FF), D ** -0.5),
        "w_exp_up": nrm(ks[22], (DEPTH, N_EXPERTS, D, EXPERT_FF), D ** -0.5),
        "w_exp_down": nrm(ks[23], (DEPTH, N_EXPERTS, EXPERT_FF, D), EXPERT_FF ** -0.5),
        "final_norm_w": 1.0 + nrm(ks[24], (D,), 0.05),
    }


def reference(x_prompt, x_sample, c_prompt, c_sample, norm1_w, norm2_w, w_ada, b_ada, w_in, b_mgate,
              conv_w, conv_b, ml_norm_w, w_ml_out, w_att_out, w_o, w_router, w_exp_gate, w_exp_up,
              w_exp_down, final_norm_w):
    y_prompt = run_trunk(x_prompt, c_prompt, norm1_w, norm2_w, w_ada, b_ada, w_in, b_mgate, conv_w, conv_b,
                         ml_norm_w, w_ml_out, w_att_out, w_o, w_router, w_exp_gate, w_exp_up, w_exp_down,
                         final_norm_w)
    y_sample = run_trunk(x_sample, c_sample, norm1_w, norm2_w, w_ada, b_ada, w_in, b_mgate, conv_w, conv_b,
                         ml_norm_w, w_ml_out, w_att_out, w_o, w_router, w_exp_gate, w_exp_up, w_exp_down,
                         final_norm_w)
    return (y_prompt, y_sample)
```

```python
import functools
import math

import jax
import jax.numpy as jnp
from jax import lax
from jax.experimental import pallas as pl
from jax.experimental.pallas import tpu as pltpu

F32 = jnp.float32
BF16 = jnp.bfloat16
I32 = jnp.int32

NORM_EPS = 1e-6
MASK_VALUE = -1e30
HEAD_DIM = 128
ATT_DILATIONS = (1, 4, 16)
ATT_WINDOWS = (128, 512, 2048)
ATT_HALF = 64
ATT_HEADS_PER_GROUP = 4
ATT_GROUP_W = ATT_HEADS_PER_GROUP * HEAD_DIM
ROT_HALF = 16
ROT_THETA = 500000.0
ML_HEADS = 4
ML_V_DIM = 256
ML_QK_W = ML_HEADS * HEAD_DIM
ML_V_W = ML_HEADS * ML_V_DIM
ML_CHUNK = 128
ML_CONV_TAPS = 5
N_EXPERTS = 16
CAPACITY_FACTOR = 2
GATE_PAD = 128
LANES = 128
PACK_W = 256
VMEM_LIMIT = 56 * 1024 * 1024
QK_SCALE = HEAD_DIM ** -0.5

PROJ_TN = 512
ATT_POS_TILE = 2048
MOE_ROWS = 256


def _cparams(sem):
    return pltpu.CompilerParams(dimension_semantics=sem, vmem_limit_bytes=VMEM_LIMIT)


def _sigmoid(x):
    return 1.0 / (1.0 + jnp.exp(-x))


def _silu(x):
    return x * _sigmoid(x)


def _bf16_bits(x):
    b = lax.bitcast_convert_type(x, I32)
    return b + 0x7FFF + (lax.shift_right_logical(b, 16) & 1)


def _pack_pair(lo, hi):
    return (_bf16_bits(hi) & (-65536)) | lax.shift_right_logical(_bf16_bits(lo), 16)


def _unpack_pair(w):
    lo = lax.bitcast_convert_type(lax.shift_left(w, 16), F32)
    hi = lax.bitcast_convert_type(w & (-65536), F32)
    return lo, hi


def _ada_kernel(c_ref, w_ref, b_ref, o_ref):
    c = c_ref[...]
    a = _silu(c).astype(BF16)
    o_ref[0] = jnp.dot(a, w_ref[0].astype(BF16), preferred_element_type=F32) + b_ref[0]


def ada_modulation(c_pad, w_ada, b_ada):
    m, d = c_pad.shape
    depth, _, n = w_ada.shape
    tn = min(n, 1536)
    return pl.pallas_call(
        _ada_kernel,
        out_shape=jax.ShapeDtypeStruct((depth, m, n), F32),
        grid=(depth, n // tn),
        in_specs=[pl.BlockSpec((m, d), lambda l, j: (0, 0)),
                  pl.BlockSpec((1, d, tn), lambda l, j: (l, 0, j)),
                  pl.BlockSpec((1, 1, tn), lambda l, j: (l, 0, j))],
        out_specs=pl.BlockSpec((1, m, tn), lambda l, j: (l, 0, j)),
        compiler_params=_cparams(("parallel", "parallel")),
    )(c_pad, w_ada, b_ada.reshape(depth, 1, n))


def _rope(a, cos, sin_signed):
    up = pltpu.roll(a, HEAD_DIM - ROT_HALF, axis=1)
    dn = pltpu.roll(a, ROT_HALF, axis=1)
    lane = lax.broadcasted_iota(I32, a.shape, 1)
    partner = jnp.where(lane < ROT_HALF, up, dn)
    return a * cos + partner * sin_signed


def _inproj_kernel(x_ref, sc_ref, sh_ref, nw_ref, cos_ref, sin_ref, w_ref, wg_ref,
                   a0_ref, a1_ref, a2_ref, qk_ref, v_ref, o_ref, mg_ref, gt_ref,
                   h_scr, acc_scr, *, tm):
    j = pl.program_id(1)

    @pl.when(j == 0)
    def _():
        x = x_ref[...]
        ms = jnp.mean(x * x, axis=-1, keepdims=True)
        h = x * lax.rsqrt(ms + NORM_EPS) * nw_ref[...]
        h = h * (1.0 + sc_ref[0]) + sh_ref[0]
        hb = h.astype(BF16)
        h_scr[...] = hb
        gt_ref[...] = jnp.dot(hb, wg_ref[...], preferred_element_type=F32)

    acc = jnp.dot(h_scr[...], w_ref[...], preferred_element_type=F32)

    att_refs = (a0_ref, a1_ref, a2_ref)
    for g, dil in enumerate(ATT_DILATIONS):
        for t in range(3):

            @pl.when(j == 3 * g + t)
            def _(g=g, dil=dil, t=t):
                a = acc
                if t < 2:
                    cos = cos_ref[...]
                    sin = sin_ref[...]
                    heads = []
                    for hh in range(ATT_HEADS_PER_GROUP):
                        ah = a[:, hh * HEAD_DIM:(hh + 1) * HEAD_DIM]
                        if t == 0:
                            ah = ah * QK_SCALE
                        heads.append(_rope(ah, cos, sin))
                    a = jnp.concatenate(heads, axis=1)
                out = att_refs[g]
                if dil == 1:
                    out[0, 0] = a.astype(BF16)
                else:
                    for hh in range(ATT_HEADS_PER_GROUP):
                        acc_scr[hh] = a[:, hh * HEAD_DIM:(hh + 1) * HEAD_DIM]
                    for rho in range(dil):
                        for hh in range(ATT_HEADS_PER_GROUP):
                            out[0, rho, :, hh * HEAD_DIM:(hh + 1) * HEAD_DIM] = (
                                acc_scr[hh, pl.ds(rho, tm // dil, stride=dil), :].astype(BF16))

    @pl.when((j == 9) | (j == 10))
    def _():
        qk_ref[...] = acc

    @pl.when((j == 11) | (j == 12))
    def _():
        v_ref[...] = acc.astype(BF16)

    @pl.when((j == 13) | (j == 14))
    def _():
        o_ref[...] = acc.astype(BF16)

    @pl.when(j >= 15)
    def _():
        mg_ref[...] = acc.astype(BF16)


def input_projection(x2d, scale, shift, norm_w, cos_t, sin_t, w_main, w_gates, *, batch, seq):
    n, d = x2d.shape
    tm = min(512, seq)
    tpb = seq // tm
    nj =w_main.shape[1] // PROJ_TN
    d_merge = w_main.shape[1] - (9 + 6) * PROJ_TN
    assert nj == 15 + d_merge // PROJ_TN and seq % tm == 0 and tm % (16 * 16) == 0

    def att_spec(g, dil):
        return pl.BlockSpec((1, dil, tm // dil, PROJ_TN),
                            lambda i, j: (i // tpb, 0, i % tpb, jnp.clip(j - 3 * g, 0, 2)))

    def col_spec(first, count):
        return pl.BlockSpec((tm, PROJ_TN), lambda i, j: (i, jnp.clip(j - first, 0, count - 1)))

    out_shape = [jax.ShapeDtypeStruct((batch, dil, seq // dil, 3 * ATT_GROUP_W), BF16) for dil in ATT_DILATIONS]
    out_shape += [jax.ShapeDtypeStruct((n, 2 * ML_QK_W), F32),
                  jax.ShapeDtypeStruct((n, ML_V_W), BF16),
                  jax.ShapeDtypeStruct((n, ML_V_W), BF16),
                  jax.ShapeDtypeStruct((n, d_merge), BF16),
                  jax.ShapeDtypeStruct((n, GATE_PAD), F32)]
    out_specs = [att_spec(g, dil) for g, dil in enumerate(ATT_DILATIONS)]
    out_specs += [col_spec(9, 2), col_spec(11, 2), col_spec(13, 2), col_spec(15, d_merge // PROJ_TN),
                  pl.BlockSpec((tm, GATE_PAD), lambda i, j: (i, 0))]
    return pl.pallas_call(
        functools.partial(_inproj_kernel, tm=tm),
        out_shape=out_shape,
        grid=(n // tm, nj),
        in_specs=[pl.BlockSpec((tm, d), lambda i, j: (i, 0)),
                  pl.BlockSpec((1, 1, d), lambda i, j: (i // tpb, 0, 0)),
                  pl.BlockSpec((1, 1, d), lambda i, j: (i // tpb, 0, 0)),
                  pl.BlockSpec((1, d), lambda i, j: (0, 0)),
                  pl.BlockSpec((tm, HEAD_DIM), lambda i, j: (i % tpb, 0)),
                  pl.BlockSpec((tm, HEAD_DIM), lambda i, j: (i % tpb, 0)),
                  pl.BlockSpec((d, PROJ_TN), lambda i, j: (0, j)),
                  pl.BlockSpec((d, GATE_PAD), lambda i, j: (0, 0))],
        out_specs=out_specs,
        scratch_shapes=[pltpu.VMEM((tm, d), BF16), pltpu.VMEM((ATT_HEADS_PER_GROUP, tm, HEAD_DIM), F32)],
        compiler_params=_cparams(("parallel", "arbitrary")),
    )(x2d, scale, shift, norm_w, cos_t, sin_t, w_main, w_gates)


def _attention_kernel(*refs, pos_tile):
    groups = [refs[7 * g:7 * g + 7] for g in range(3)]
    o_ref = refs[21]
    kcat, vcat, acc_n, m_n, l_n = refs[22:27]
    i = pl.program_id(2)
    last = pl.num_programs(2) - 1
    sub = ML_CHUNK
    row = lax.broadcasted_iota(I32, (sub, 2 * sub), 0)
    col = lax.broadcasted_iota(I32, (sub, 2 * sub), 1)
    band = (col >= row) & (col - row <= 2 * ATT_HALF)
    for g, dil in enumerate(ATT_DILATIONS):
        q_ref, kp_ref, kc_ref, kn_ref, vp_ref, vc_ref, vn_ref = groups[g]
        r = pos_tile // dil
        for rho in range(dil):
            kcat[0:ATT_HALF] = kp_ref[0, rho]
            kcat[ATT_HALF:ATT_HALF + r] = kc_ref[0, rho]
            kcat[ATT_HALF + r:2 * ATT_HALF + r] = kn_ref[0, rho]
            vcat[0:ATT_HALF] = vp_ref[0, rho]
            vcat[ATT_HALF:ATT_HALF + r] = vc_ref[0, rho]
            vcat[ATT_HALF + r:2 * ATT_HALF + r] = vn_ref[0, rho]
            for s in range(r // sub):
                q = q_ref[0, rho, s * sub:(s + 1) * sub, :]
                k = kcat[s * sub:(s + 2) * sub, :]
                v = vcat[s * sub:(s + 2) * sub, :]
                sc = lax.dot_general(q, k, (((1,), (1,)), ((), ())), preferred_element_type=F32)
                kidx = col + s * sub
                ok = band & ((kidx >= ATT_HALF) | (i > 0)) & ((kidx < r + ATT_HALF) | (i < last))
                sc = jnp.where(ok, sc, MASK_VALUE)
                m = jnp.max(sc, axis=-1, keepdims=True)
                p = jnp.exp(sc - m)
                l = jnp.sum(p, axis=-1, keepdims=True)
                acc = jnp.dot(p.astype(BF16), v, preferred_element_type=F32)
                start = rho + dil * sub * s
                if dil == 1:
                    dst = pl.ds(start, sub)
                else:
                    dst = pl.ds(start, sub, stride=dil)
                acc_n[g, dst, :] = acc
                m_n[g, dst, :] = jnp.broadcast_to(m, (sub, HEAD_DIM))
                l_n[g, dst, :] = jnp.broadcast_to(l, (sub, HEAD_DIM))
    m_all = jnp.maximum(jnp.maximum(m_n[0], m_n[1]), m_n[2])
    num = jnp.zeros_like(m_all)
    den = jnp.zeros_like(m_all)
    for g in range(3):
        w = jnp.exp(m_n[g] - m_all)
        num = num + w * acc_n[g]
        den = den + w * l_n[g]
    o_ref[...] = (num / den).astype(BF16)


def attention_branch(att_groups, *, batch, seq):
    p = min(ATT_POS_TILE, seq)
    assert seq % p == 0 and p % (ML_CHUNK * ATT_DILATIONS[-1]) == 0
    nblk = seq // p
    hb = ATT_HALF
    in_specs = []
    args = []
    for g, dil in enumerate(ATT_DILATIONS):
        r = p // dil
        s_len = seq // dil
        n_half_blocks = s_len // hb

        def cur(col0, r=r, dil=dil):
            return pl.BlockSpec((1, dil, r, HEAD_DIM), lambda b, h, i: (b, 0, i, col0 + h))

        def prev(col0, r=r, dil=dil):
            return pl.BlockSpec((1, dil, hb, HEAD_DIM),
                                lambda b, h, i: (b, 0, jnp.maximum(i * (r // hb) - 1, 0), col0 + h))

        def nxt(col0, r=r, dil=dil, nhb=n_half_blocks):
            return pl.BlockSpec((1, dil, hb, HEAD_DIM),
                                lambda b, h, i: (b, 0, jnp.minimum((i + 1) * (r // hb), nhb - 1), col0 + h))

        kc0, vc0 = ATT_HEADS_PER_GROUP, 2 * ATT_HEADS_PER_GROUP
        in_specs += [cur(0), prev(kc0), cur(kc0), nxt(kc0), prev(vc0), cur(vc0), nxt(vc0)]
        args += [att_groups[g]] * 7
    return pl.pallas_call(
        functools.partial(_attention_kernel, pos_tile=p),
        out_shape=jax.ShapeDtypeStruct((batch * seq, ATT_GROUP_W), BF16),
        grid=(batch, ATT_HEADS_PER_GROUP, nblk),
        in_specs=in_specs,
        out_specs=pl.BlockSpec((p, HEAD_DIM), lambda b, h, i: (b * nblk + i, h)),
        scratch_shapes=[pltpu.VMEM((p + 2 * hb, HEAD_DIM), BF16),
                        pltpu.VMEM((p + 2 * hb, HEAD_DIM), BF16),
                        pltpu.VMEM((3, p, HEAD_DIM), F32),
                        pltpu.VMEM((3, p, HEAD_DIM), F32),
                        pltpu.VMEM((3, p, HEAD_DIM), F32)],
        compiler_params=_cparams(("parallel", "parallel", "arbitrary")),
    )(*args)


def _conv_kernel(prev_ref, cur_ref, next_ref, w_ref, b_ref, o_ref, xcat, *, tt, tpb):
    i = pl.program_id(0)
    first = (i % tpb) == 0
    last = (i % tpb) == tpb - 1
    xcat[0:8] = jnp.where(first, 0.0, prev_ref[...])
    xcat[8:8 + tt] = cur_ref[...]
    xcat[8 + tt:16 + tt] = jnp.where(last, 0.0, next_ref[...])
    acc = jnp.broadcast_to(b_ref[...], (tt, b_ref.shape[1]))
    for j in range(ML_CONV_TAPS):
        acc = acc + w_ref[j:j + 1, :] * xcat[pl.ds(8 - ML_CONV_TAPS // 2 + j, tt), :]
    y = _silu(acc)
    lane = lax.broadcasted_iota(I32, y.shape, 1)
    o_ref[...] = jnp.where(lane < ML_QK_W, y * QK_SCALE, y).astype(BF16)


def mlstm_conv(qk_raw, conv_w, conv_b, *, seq):
    n, c = qk_raw.shape
    tt = min(512, seq)
    tpb = seq // tt
    w_pad = jnp.zeros((8, c), F32).at[:ML_CONV_TAPS].set(conv_w)
    return pl.pallas_call(
        functools.partial(_conv_kernel, tt=tt, tpb=tpb),
        out_shape=jax.ShapeDtypeStruct((n, c), BF16),
        grid=(n // tt,),
        in_specs=[pl.BlockSpec((8, c), lambda i: (jnp.maximum(i * (tt // 8) - 1, 0), 0)),
                  pl.BlockSpec((tt, c), lambda i: (i, 0)),
                  pl.BlockSpec((8, c), lambda i: (jnp.minimum((i + 1) * (tt // 8), n // 8 - 1), 0)),
                  pl.BlockSpec((8, c), lambda i: (0, 0)),
                  pl.BlockSpec((1, c), lambda i: (0, 0))],
        out_specs=pl.BlockSpec((tt, c), lambda i: (i, 0)),
        scratch_shapes=[pltpu.VMEM((tt + 16, c), F32)],
        compiler_params=_cparams(("parallel",)),
    )(qk_raw, qk_raw, qk_raw, w_pad, conv_b.reshape(1, c))


def _log_sigmoid(x):
    return jnp.minimum(x, 0.0) - jnp.log(1.0 + jnp.exp(-jnp.abs(x)))


def _mlstm_kernel(q_ref, k_ref, v_ref, g_ref, bias_ref, h_ref, c_scr, n_scr, m_scr, *, reverse):
    cidx = pl.program_id(1)
    L = ML_CHUNK

    @pl.when(cidx == 0)
    def _():
        c_scr[...] = jnp.zeros_like(c_scr)
        n_scr[...] = jnp.zeros_like(n_scr)
        m_scr[...] = jnp.zeros_like(m_scr)

    row = lax.broadcasted_iota(I32, (L, L), 0)
    col = lax.broadcasted_iota(I32, (L, L), 1)
    causal = (col >= row) if reverse else (col <= row)
    gates = g_ref[...] + bias_ref[...]
    logf = _log_sigmoid(gates)
    b_cols = jnp.dot(causal.astype(F32), logf, preferred_element_type=F32,
                     precision=lax.Precision.HIGHEST)
    gates_t = gates.T
    b_rows = b_cols.T
    edge = 0 if reverse else L - 1
    off = 2 * ML_HEADS if reverse else 0
    for h in range(ML_HEADS):
        ci = off + h
        cf = off + ML_HEADS + h
        q = q_ref[:, h * HEAD_DIM:(h + 1) * HEAD_DIM]
        k = k_ref[:, h * HEAD_DIM:(h + 1) * HEAD_DIM]
        v = v_ref[:, h * ML_V_DIM:(h + 1) * ML_V_DIM]
        ib_r = gates_t[ci:ci + 1, :]
        ib_c = gates[:, ci:ci + 1]
        b_r = b_rows[cf:cf + 1, :]
        b_c = b_cols[:, cf:cf + 1]
        b_last = b_cols[edge:edge + 1, cf:cf + 1]
        m_prev = m_scr[h][:, 0:1]
        c_prev = c_scr[h]
        n_prev = n_scr[h]
        dlog = jnp.where(causal, b_c - b_r + ib_r, MASK_VALUE)
        inter = b_c + m_prev
        m_i = jnp.maximum(inter, jnp.max(dlog, axis=-1, keepdims=True))
        w = jnp.exp(dlog - m_i)
        s = lax.dot_general(q, k, (((1,), (1,)), ((), ())), preferred_element_type=F32) * w
        sc = jnp.exp(inter - m_i)
        num = jnp.dot(s.astype(BF16), v, preferred_element_type=F32)
        num = num + sc * jnp.dot(q, c_prev.astype(BF16), preferred_element_type=F32)
        qn = jnp.sum(q.astype(F32) * n_prev, axis=-1, keepdims=True)
        den = jnp.sum(s, axis=-1, keepdims=True) + sc * qn
        h_ref[:, h * ML_V_DIM:(h + 1) * ML_V_DIM] = num / jnp.maximum(jnp.abs(den), jnp.exp(-m_i))
        g_r = b_last - b_r + ib_r
        g_c = b_last - b_c + ib_c
        m_new = jnp.maximum(b_last + m_prev, jnp.max(g_r, axis=-1, keepdims=True))
        decay = jnp.exp(b_last + m_prev - m_new)
        kw = k.astype(F32) * jnp.exp(g_c - m_new)
        c_scr[h] = decay * c_prev + lax.dot_general(kw.astype(BF16), v, (((0,), (0,)), ((), ())),
                                                    preferred_element_type=F32)
        n_scr[h] = decay * n_prev + jnp.sum(kw, axis=0, keepdims=True)
        m_scr[h] = jnp.broadcast_to(m_new, (1, LANES))


def mlstm_scan(qk, v, gates, bias_pad, *, batch, seq, reverse):
    n = batch * seq
    nc = seq // ML_CHUNK

    def chunk(b, c):
        return b * nc + ((nc - 1 - c) if reverse else c)

    return pl.pallas_call(
        functools.partial(_mlstm_kernel, reverse=reverse),
        out_shape=jax.ShapeDtypeStruct((n, ML_V_W), F32),
        grid=(batch, nc),
        in_specs=[pl.BlockSpec((ML_CHUNK, ML_QK_W), lambda b, c: (chunk(b, c), 0)),
                  pl.BlockSpec((ML_CHUNK, ML_QK_W), lambda b, c: (chunk(b, c), 1)),
                  pl.BlockSpec((ML_CHUNK, ML_V_W), lambda b, c: (chunk(b, c), 0)),
                  pl.BlockSpec((ML_CHUNK, GATE_PAD), lambda b, c: (chunk(b, c), 0)),
                  pl.BlockSpec((1, GATE_PAD), lambda b, c: (0, 0))],
        out_specs=pl.BlockSpec((ML_CHUNK, ML_V_W), lambda b, c: (chunk(b, c), 0)),
        scratch_shapes=[pltpu.VMEM((ML_HEADS, HEAD_DIM, ML_V_DIM), F32),
                        pltpu.VMEM((ML_HEADS, 1, HEAD_DIM), F32),
                        pltpu.VMEM((ML_HEADS, 1, LANES), F32)],
        compiler_params=_cparams(("parallel", "arbitrary")),
    )(qk, qk, v, gates, bias_pad)


def _postmix_kernel(att_ref, hf_ref, hb_ref, mlo_ref, mg_ref, x_ref, g1_ref, sc2_ref, sh2_ref,
                    n2w_ref, mlnw_ref, wml_ref, watt_ref, wo_ref, wr_ref,
                    x1_ref, h2_ref, lg_ref, *, d):
    hm = hf_ref[...] + hb_ref[...]
    parts = []
    for h in range(ML_HEADS):
        hh = hm[:, h * ML_V_DIM:(h + 1) * ML_V_DIM]
        parts.append(hh * lax.rsqrt(jnp.mean(hh * hh, axis=-1, keepdims=True) + NORM_EPS))
    hn = jnp.concatenate(parts, axis=1) * mlnw_ref[...]
    ml_act = (_sigmoid(mlo_ref[...].astype(F32)) * hn).astype(BF16)
    y_ml = jnp.dot(ml_act, wml_ref[...], preferred_element_type=F32)
    y_att = jnp.dot(att_ref[...], watt_ref[...], preferred_element_type=F32)
    g_ml = _sigmoid(mg_ref[:, 0:d].astype(F32))
    g_att = _sigmoid(mg_ref[:, d:2 * d].astype(F32))
    merged = (g_ml * y_ml + g_att * y_att).astype(BF16)
    out = jnp.dot(merged, wo_ref[...], preferred_element_type=F32)
    x1 = x_ref[...] + g1_ref[0] * out
    x1_ref[...] = x1
    ms = jnp.mean(x1 * x1, axis=-1, keepdims=True)
    h2 = x1 * lax.rsqrt(ms + NORM_EPS) * n2w_ref[...]
    h2 = h2 * (1.0 + sc2_ref[0]) + sh2_ref[0]
    for c in range(d // PACK_W):
        lo = h2[:, c * PACK_W:c * PACK_W + LANES]
        hi = h2[:, c * PACK_W + LANES:(c + 1) * PACK_W]
        h2_ref[:, c, :] = _pack_pair(lo, hi)
    lg_ref[...] = jnp.dot(h2, wr_ref[...], preferred_element_type=F32, precision=lax.Precision.HIGHEST)


def post_mixer(att_o, h_f, h_b, ml_o, merge, x2d, gate1, scale2, shift2, norm2_w, ml_norm_w,
               w_ml_out, w_att_out, w_o, w_router_pad, *, seq):
    n, d = x2d.shape
    tm = 256
    tpb = seq // tm
    const = lambda shape: pl.BlockSpec(shape, lambda i: tuple(0 for _ in shape), pipeline_mode=pl.Buffered(1))
    rows = lambda w: pl.BlockSpec((tm, w), lambda i: (i, 0))
    per_b = pl.BlockSpec((1, 1, d), lambda i: (i // tpb, 0, 0))
    return pl.pallas_call(
        functools.partial(_postmix_kernel, d=d),
        out_shape=[jax.ShapeDtypeStruct((n, d), F32),
                   jax.ShapeDtypeStruct((n, d // PACK_W, LANES), I32),
                   jax.ShapeDtypeStruct((n, LANES), F32)],
        grid=(n // tm,),
        in_specs=[rows(ATT_GROUP_W), rows(ML_V_W), rows(ML_V_W), rows(ML_V_W), rows(2 * d), rows(d),
                  per_b, per_b, per_b, const((1, d)), const((1, ML_V_W)),
                  const((ML_V_W, d)), const((ATT_GROUP_W, d)), const((d, d)), const((d, LANES))],
        out_specs=[rows(d),
                   pl.BlockSpec((tm, d // PACK_W, LANES), lambda i: (i, 0, 0)),
                   rows(LANES)],
        compiler_params=_cparams(("parallel",)),
    )(att_o, h_f, h_b, ml_o, merge, x2d, gate1, scale2, shift2, norm2_w, ml_norm_w,
      w_ml_out, w_att_out, w_o, w_router_pad)


def _expert_kernel(idx_ref, idxn_ref, dst_ref, gate_ref, wg_ref, wu_ref, wd_ref, h2_hbm, z_hbm,
                   xbuf, zbuf, xs, acc, gsem, ssem, *, d, ff, tr):
    s = pl.program_id(0)
    nsteps = pl.num_programs(0)
    slot = s % 2

    def start_gather(ids, to_slot):
        def body(r, carry):
            pltpu.make_async_copy(h2_hbm.at[ids[0, 0, r]], xbuf.at[to_slot, r], gsem.at[to_slot]).start()
            return carry
        lax.fori_loop(0, tr, body, 0)

    @pl.when(s == 0)
    def _():
        start_gather(idx_ref, 0)

    @pl.when(s + 1 < nsteps)
    def _():
        start_gather(idxn_ref, 1 - slot)

    pltpu.make_async_copy(h2_hbm.at[pl.ds(0, tr)], xbuf.at[slot], gsem.at[slot]).wait()

    for c in range(d // PACK_W):
        lo, hi = _unpack_pair(xbuf[slot, :, c, :])
        xs[:, c * PACK_W:c * PACK_W + LANES] = lo.astype(BF16)
        xs[:, c * PACK_W + LANES:(c + 1) * PACK_W] = hi.astype(BF16)

    x = xs[...]
    tf = min(ff, 512)
    for f in range(ff // tf):
        g = jnp.dot(x, wg_ref[0, :, f * tf:(f + 1) * tf], preferred_element_type=F32)
        u = jnp.dot(x, wu_ref[0, :, f * tf:(f + 1) * tf], preferred_element_type=F32)
        hid = (_silu(g) * u).astype(BF16)
        part = jnp.dot(hid, wd_ref[0, f * tf:(f + 1) * tf, :], preferred_element_type=F32)
        if f == 0:
            acc[...] = part
        else:
            acc[...] += part

    def wait_scatter():
        pltpu.make_async_copy(zbuf, z_hbm.at[pl.ds(0, tr)], ssem.at[0]).wait()

    @pl.when(s > 0)
    def _():
        wait_scatter()

    gate = gate_ref[...]
    for c in range(d // PACK_W):
        lo = acc[:, c * PACK_W:c * PACK_W + LANES] * gate
        hi = acc[:, c * PACK_W + LANES:(c + 1) * PACK_W] * gate
        zbuf[:, c, :] = _pack_pair(lo, hi)

    def scatter_body(r, carry):
        pltpu.make_async_copy(zbuf.at[r], z_hbm.at[dst_ref[0, 0, r]], ssem.at[0]).start()
        return carry
    lax.fori_loop(0, tr, scatter_body, 0)

    @pl.when(s == nsteps - 1)
    def _():
        wait_scatter()


def expert_ffn(h2_packed, idx, dest, gate_rows, w_gate, w_up, w_down):
    n, nchunk, _ = h2_packed.shape
    e, d, ff = w_gate.shape
    total = idx.shape[0]
    cap = total // e
    tr = min(MOE_ROWS, cap)
    nsteps = total // tr
    per_e = cap // tr
    idx3 = idx.reshape(nsteps, 1, tr)
    dst3 = dest.reshape(nsteps, 1, tr)
    smem = lambda f: pl.BlockSpec((1, 1, tr), f, memory_space=pltpu.SMEM)
    return pl.pallas_call(
        functools.partial(_expert_kernel, d=d, ff=ff, tr=tr),
        out_shape=jax.ShapeDtypeStruct((total, nchunk, LANES), I32),
        grid=(nsteps,),
        in_specs=[smem(lambda s: (s, 0, 0)),
                  smem(lambda s: (jnp.minimum(s + 1, nsteps - 1), 0, 0)),
                  smem(lambda s: (s, 0, 0)),
                  pl.BlockSpec((tr, LANES), lambda s: (s, 0)),
                  pl.BlockSpec((1, d, ff), lambda s: (s // per_e, 0, 0), pipeline_mode=pl.Buffered(1)),
                  pl.BlockSpec((1, d, ff), lambda s: (s // per_e, 0, 0), pipeline_mode=pl.Buffered(1)),
                  pl.BlockSpec((1, ff, d), lambda s: (s // per_e, 0, 0), pipeline_mode=pl.Buffered(1)),
                  pl.BlockSpec(memory_space=pl.ANY)],
        out_specs=pl.BlockSpec(memory_space=pl.ANY),
        scratch_shapes=[pltpu.VMEM((2, tr, nchunk, LANES), I32),
                        pltpu.VMEM((tr, nchunk, LANES), I32),
                        pltpu.VMEM((tr, d), BF16),
                        pltpu.VMEM((tr, d), F32),
                        pltpu.SemaphoreType.DMA((2,)),
                        pltpu.SemaphoreType.DMA((1,))],
        compiler_params=_cparams(("arbitrary",)),
    )(idx3, idx3, dst3, gate_rows, w_gate, w_up, w_down, h2_packed)


def _combine_kernel(cs_ref, cn_ref, off_ref, cnt_ref, x_ref, g2_ref, fw_ref, z_hbm, o_ref,
                    zbuf, zs, acc, sem, *, d, tm, tr, final):
    t = pl.program_id(0)
    c0 = cs_ref[t]
    ncnk = cn_ref[t]
    acc[...] = jnp.zeros_like(acc)
    off = off_ref[0]
    cnt = cnt_ref[0]
    zrow = lax.broadcasted_iota(I32, (tr, tm), 0)

    def body(ci, carry):
        base = (c0 + ci) * tr
        cp = pltpu.make_async_copy(z_hbm.at[pl.ds(base, tr)], zbuf, sem.at[0])
        cp.start()
        cp.wait()
        for c in range(d // PACK_W):
            lo, hi = _unpack_pair(zbuf[:, c, :])
            zs[:, c * PACK_W:c * PACK_W + LANES] = lo.astype(BF16)
            zs[:, c * PACK_W + LANES:(c + 1) * PACK_W] = hi.astype(BF16)
        z = zrow + base
        sel = jnp.where((z >= off) & (z < off + cnt), 1.0, 0.0).astype(BF16)
        acc[...] += lax.dot_general(sel, zs[...], (((0,), (0,)), ((), ())), preferred_element_type=F32)
        return carry

    lax.fori_loop(0, ncnk, body, 0)
    x2 = x_ref[...] + g2_ref[0] * acc[...]
    if final:
        ms = jnp.mean(x2 * x2, axis=-1, keepdims=True)
        x2 = x2 * lax.rsqrt(ms + NORM_EPS) * fw_ref[...]
    o_ref[...] = x2


def moe_combine(z_sorted, chunk_start, chunk_count, tok_off, tok_cnt, x1, gate2, final_w, *, seq, final):
    n, d = x1.shape
    total, nchunk, _ = z_sorted.shape
    tm = min(MOE_ROWS, n)
    tr = min(MOE_ROWS, total)
    tpb = seq // tm
    nt = n // tm
    return pl.pallas_call(
        functools.partial(_combine_kernel, d=d, tm=tm, tr=tr, final=final),
        out_shape=jax.ShapeDtypeStruct((n, d), F32),
        grid_spec=pltpu.PrefetchScalarGridSpec(
            num_scalar_prefetch=2,
            grid=(nt,),
            in_specs=[pl.BlockSpec((1, 1, tm), lambda t, cs, cn: (t, 0, 0)),
                      pl.BlockSpec((1, 1, tm), lambda t, cs, cn: (t, 0, 0)),
                      pl.BlockSpec((tm, d), lambda t, cs, cn: (t, 0)),
                      pl.BlockSpec((1, 1, d), lambda t, cs, cn: (t // tpb, 0, 0)),
                      pl.BlockSpec((1, d), lambda t, cs, cn: (0, 0)),
                      pl.BlockSpec(memory_space=pl.ANY)],
            out_specs=pl.BlockSpec((tm, d), lambda t, cs, cn: (t, 0)),
            scratch_shapes=[pltpu.VMEM((tr, nchunk, LANES), I32),
                            pltpu.VMEM((tr, d), BF16),
                            pltpu.VMEM((tm, d), F32),
                            pltpu.SemaphoreType.DMA((1,))]),
        compiler_params=_cparams(("arbitrary",)),
    )(chunk_start, chunk_count, tok_off.reshape(nt, 1, tm), tok_cnt.reshape(nt, 1, tm),
      x1, gate2, final_w, z_sorted)


def route_tokens(logits, n_experts):
    n = logits.shape[0]
    cap = CAPACITY_FACTOR * n // n_experts
    aff = jax.nn.softmax(logits[:, :n_experts], axis=-1)
    gate_vals, idx = lax.top_k(aff.T, cap)
    flat = idx.reshape(-1)
    order = jnp.argsort(flat, stable=True)
    dest = jnp.zeros_like(order).at[order].set(jnp.arange(order.shape[0], dtype=order.dtype))
    cnt = jnp.zeros((n,), I32).at[flat].add(1)
    off = jnp.cumsum(cnt) - cnt
    gate_rows = jnp.broadcast_to(gate_vals.reshape(-1, 1), (flat.shape[0], LANES))
    return flat.astype(I32), dest.astype(I32), gate_rows, off.astype(I32), cnt


def _rope_tables(seq):
    freqs = jnp.float32(ROT_THETA) ** (-jnp.arange(ROT_HALF, dtype=F32) / ROT_HALF)
    ang = jnp.arange(seq, dtype=F32)[:, None] * freqs[None, :]
    cos = jnp.cos(ang)
    sin = jnp.sin(ang)
    ones = jnp.ones((seq, HEAD_DIM - 2 * ROT_HALF), F32)
    cos_t = jnp.concatenate([cos, cos, ones], axis=1)
    sin_t = jnp.concatenate([-sin, sin, 0.0 * ones], axis=1)
    return cos_t, sin_t


def _prep_layer(p, l):
    w_in = p["w_in"][l]
    aw = 3 * ATT_GROUP_W
    q, k, v = w_in[:, 0:aw], w_in[:, aw:2 * aw], w_in[:, 2 * aw:3 * aw]
    cols = []
    for g in range(3):
        sl = slice(g * ATT_GROUP_W, (g + 1) * ATT_GROUP_W)
        cols += [q[:, sl], k[:, sl], v[:, sl]]
    o = 3 * aw
    cols.append(w_in[:, o:o + 2 * ML_QK_W + 2 * ML_V_W])
    o += 2 * ML_QK_W + 2 * ML_V_W
    w_gates = w_in[:, o:o + 4 * ML_HEADS]
    o += 4 * ML_HEADS
    cols.append(w_in[:, o:])
    d = w_in.shape[0]
    return dict(
        w_main=jnp.concatenate(cols, axis=1).astype(BF16),
        w_gates=jnp.zeros((d, GATE_PAD), BF16).at[:, :4 * ML_HEADS].set(w_gates.astype(BF16)),
        bias_pad=jnp.zeros((1, GATE_PAD), F32).at[0, :4 * ML_HEADS].set(p["b_mgate"][l]),
        w_ml_out=p["w_ml_out"][l].astype(BF16),
        w_att_out=p["w_att_out"][l].astype(BF16),
        w_o=p["w_o"][l].astype(BF16),
        w_router=jnp.zeros((d, LANES), F32).at[:, :N_EXPERTS].set(p["w_router"][l]),
        w_exp_gate=p["w_exp_gate"][l].astype(BF16),
        w_exp_up=p["w_exp_up"][l].astype(BF16),
        w_exp_down=p["w_exp_down"][l].astype(BF16),
    )


def _run_group(x, mods, layers, p, cos_t, sin_t):
    batch, seq, d = x.shape
    n = batch * seq
    x2d = x.reshape(n, d)
    depth = len(layers)
    for l, lw in enumerate(layers):
        shift1, scale1, gate1, shift2, scale2, gate2 = [m.reshape(batch, 1, d) for m in jnp.split(mods[l], 6, axis=-1)]
        a0, a1, a2, ml_qk, ml_v, ml_o, merge, gates = input_projection(
            x2d, scale1, shift1, p["norm1_w"][l].reshape(1, d), cos_t, sin_t, lw["w_main"], lw["w_gates"],
            batch=batch, seq=seq)
        att_o = attention_branch((a0, a1, a2), batch=batch, seq=seq)
        qk = mlstm_conv(ml_qk, p["conv_w"][l], p["conv_b"][l], seq=seq)
        h_f = mlstm_scan(qk, ml_v, gates, lw["bias_pad"], batch=batch, seq=seq, reverse=False)
        h_b = mlstm_scan(qk, ml_v, gates, lw["bias_pad"], batch=batch, seq=seq, reverse=True)
        x1, h2p, logits = post_mixer(att_o, h_f, h_b, ml_o, merge, x2d, gate1, scale2, shift2,
                                     p["norm2_w"][l].reshape(1, d), p["ml_norm_w"][l].reshape(1, ML_V_W),
                                     lw["w_ml_out"], lw["w_att_out"], lw["w_o"], lw["w_router"], seq=seq)
        idx, dest, gate_rows, off, cnt = route_tokens(logits, N_EXPERTS)
        z = expert_ffn(h2p, idx, dest, gate_rows, lw["w_exp_gate"], lw["w_exp_up"], lw["w_exp_down"])
        tm = min(MOE_ROWS, n)
        tr = min(MOE_ROWS, z.shape[0])
        tile_first = off.reshape(n // tm, tm)[:, 0]
        tile_end = jnp.concatenate([tile_first[1:], jnp.array([z.shape[0]], I32)])
        c_start = tile_first // tr
        c_count = jnp.where(tile_end > tile_first, (tile_end + tr - 1) // tr - c_start, 0)
        x2d = moe_combine(z, c_start.astype(I32), c_count.astype(I32), off, cnt, x1, gate2,
                          p["final_norm_w"].reshape(1, d), seq=seq, final=(l == depth - 1))
    return x2d.reshape(batch, seq, d)


def kernel(x_prompt, x_sample, c_prompt, c_sample, norm1_w, norm2_w, w_ada, b_ada, w_in, b_mgate, conv_w, conv_b, ml_norm_w, w_ml_out, w_att_out, w_o, w_router, w_exp_gate, w_exp_up, w_exp_down, final_norm_w):
    p = dict(norm1_w=norm1_w, norm2_w=norm2_w, w_in=w_in, b_mgate=b_mgate, conv_w=conv_w, conv_b=conv_b,
             ml_norm_w=ml_norm_w, w_ml_out=w_ml_out, w_att_out=w_att_out, w_o=w_o, w_router=w_router,
             w_exp_gate=w_exp_gate, w_exp_up=w_exp_up, w_exp_down=w_exp_down, final_norm_w=final_norm_w)
    depth = w_in.shape[0]
    bp, bs = c_prompt.shape[0], c_sample.shape[0]
    m_pad = -(-(bp + bs) // 8) * 8
    c_pad = jnp.zeros((m_pad, c_prompt.shape[1]), F32).at[:bp].set(c_prompt).at[bp:bp + bs].set(c_sample)
    mods = ada_modulation(c_pad, w_ada, b_ada)
    layers = [_prep_layer(p, l) for l in range(depth)]
    outs = []
    for x, rows in ((x_prompt, slice(0, bp)), (x_sample, slice(bp, bp + bs))):
        cos_t, sin_t = _rope_tables(x.shape[1])
        outs.append(_run_group(x, mods[:, rows], layers, p, cos_t, sin_t))
    return tuple(outs)
```

```python
import functools
import math

import jax
import jax.numpy as jnp
from jax import lax
from jax.experimental import pallas as pl
from jax.experimental.pallas import tpu as pltpu

F32 = jnp.float32
BF16 = jnp.bfloat16
I32 = jnp.int32

NORM_EPS = 1e-6
MASK_VALUE = -1e30
HEAD_DIM = 128
ATT_DILATIONS = (1, 4, 16)
ATT_WINDOWS = (128, 512, 2048)
ATT_HALF = 64
ATT_HEADS_PER_GROUP = 4
ATT_GROUP_W = ATT_HEADS_PER_GROUP * HEAD_DIM
ROT_HALF = 16
ROT_THETA = 500000.0
ML_HEADS = 4
ML_V_DIM = 256
ML_QK_W = ML_HEADS * HEAD_DIM
ML_V_W = ML_HEADS * ML_V_DIM
ML_CHUNK = 128
ML_CONV_TAPS = 5
N_EXPERTS = 16
CAPACITY_FACTOR = 2
GATE_PAD = 128
LANES = 128
VMEM_LIMIT = 56 * 1024 * 1024
QK_SCALE = HEAD_DIM ** -0.5

PROJ_TN = 512
ATT_POS_TILE = 2048
MOE_ROWS = 256
EXPERT_ROWS = 512


def _cparams(sem):
    return pltpu.CompilerParams(dimension_semantics=sem, vmem_limit_bytes=VMEM_LIMIT)


def _sigmoid(x):
    return 1.0 / (1.0 + jnp.exp(-x))


def _silu(x):
    return x * _sigmoid(x)


def _bf16_bits(x):
    b = lax.bitcast_convert_type(x, I32)
    return b + 0x7FFF + (lax.shift_right_logical(b, 16) & 1)


def _pack_pair(lo, hi):
    return (_bf16_bits(hi) & (-65536)) | lax.shift_right_logical(_bf16_bits(lo), 16)


def _unpack_pair(w):
    lo = lax.bitcast_convert_type(lax.shift_left(w, 16), F32)
    hi = lax.bitcast_convert_type(w & (-65536), F32)
    return lo, hi


def _ada_kernel(c_ref, w_ref, b_ref, o_ref):
    c = c_ref[...]
    a = _silu(c).astype(BF16)
    o_ref[0] = jnp.dot(a, w_ref[0].astype(BF16), preferred_element_type=F32) + b_ref[0]


def ada_modulation(c_pad, w_ada, b_ada):
    m, d = c_pad.shape
    depth, _, n = w_ada.shape
    tn = min(n, 1536)
    return pl.pallas_call(
        _ada_kernel,
        out_shape=jax.ShapeDtypeStruct((depth, m, n), F32),
        grid=(depth, n // tn),
        in_specs=[pl.BlockSpec((m, d), lambda l, j: (0, 0)),
                  pl.BlockSpec((1, d, tn), lambda l, j: (l, 0, j)),
                  pl.BlockSpec((1, 1, tn), lambda l, j: (l, 0, j))],
        out_specs=pl.BlockSpec((1, m, tn), lambda l, j: (l, 0, j)),
        compiler_params=_cparams(("parallel", "parallel")),
    )(c_pad, w_ada, b_ada.reshape(depth, 1, n))


def _rope(a, cos, sin_signed):
    up = pltpu.roll(a, HEAD_DIM - ROT_HALF, axis=1)
    dn = pltpu.roll(a, ROT_HALF, axis=1)
    lane = lax.broadcasted_iota(I32, a.shape, 1)
    partner = jnp.where(lane < ROT_HALF, up, dn)
    return a * cos + partner * sin_signed


def _inproj_kernel(x_ref, sc_ref, sh_ref, nw_ref, cos_ref, sin_ref, w_ref, wg_ref,
                   a0_ref, a1_ref, a2_ref, qk_ref, v_ref, o_ref, mg_ref, gt_ref,
                   h_scr, acc_scr, *, tm):
    j = pl.program_id(1)

    @pl.when(j == 0)
    def _():
        x = x_ref[...]
        ms = jnp.mean(x * x, axis=-1, keepdims=True)
        h = x * lax.rsqrt(ms + NORM_EPS) * nw_ref[...]
        h = h * (1.0 + sc_ref[0]) + sh_ref[0]
        hb = h.astype(BF16)
        h_scr[...] = hb
        gt_ref[...] = jnp.dot(hb, wg_ref[...], preferred_element_type=F32)

    acc = jnp.dot(h_scr[...], w_ref[...], preferred_element_type=F32)

    att_refs = (a0_ref, a1_ref, a2_ref)
    for g, dil in enumerate(ATT_DILATIONS):
        for t in range(3):

            @pl.when(j == 3 * g + t)
            def _(g=g, dil=dil, t=t):
                a = acc
                if t < 2:
                    cos = cos_ref[...]
                    sin = sin_ref[...]
                    heads = []
                    for hh in range(ATT_HEADS_PER_GROUP):
                        ah = a[:, hh * HEAD_DIM:(hh + 1) * HEAD_DIM]
                        if t == 0:
                            ah = ah * QK_SCALE
                        heads.append(_rope(ah, cos, sin))
                    a = jnp.concatenate(heads, axis=1)
                out = att_refs[g]
                if dil == 1:
                    out[0, 0] = a.astype(BF16)
                else:
                    for hh in range(ATT_HEADS_PER_GROUP):
                        acc_scr[hh] = a[:, hh * HEAD_DIM:(hh + 1) * HEAD_DIM]
                    for rho in range(dil):
                        for hh in range(ATT_HEADS_PER_GROUP):
                            out[0, rho, :, hh * HEAD_DIM:(hh + 1) * HEAD_DIM] = (
                                acc_scr[hh, pl.ds(rho, tm // dil, stride=dil), :].astype(BF16))

    @pl.when((j == 9) | (j == 10))
    def _():
        qk_ref[...] = acc

    @pl.when((j == 11) | (j == 12))
    def _():
        v_ref[...] = acc.astype(BF16)

    @pl.when((j == 13) | (j == 14))
    def _():
        o_ref[...] = acc.astype(BF16)

    @pl.when(j >= 15)
    def _():
        mg_ref[...] = acc.astype(BF16)


def input_projection(x2d, scale, shift, norm_w, cos_t, sin_t, w_main, w_gates, *, batch, seq):
    n, d = x2d.shape
    tm = min(1024, seq)
    tpb = seq // tm
    nj = w_main.shape[1] // PROJ_TN
    d_merge = w_main.shape[1] - (9 + 6) * PROJ_TN
    assert nj == 15 + d_merge // PROJ_TN and seq % tm == 0 and tm % (16 * 16) == 0

    def att_spec(g, dil):
        return pl.BlockSpec((1, dil, tm // dil, PROJ_TN),
                            lambda i, j: (i // tpb, 0, i % tpb, jnp.clip(j - 3 * g, 0, 2)))

    def col_spec(first, count):
        return pl.BlockSpec((tm, PROJ_TN), lambda i, j: (i, jnp.clip(j - first, 0, count - 1)))

    out_shape = [jax.ShapeDtypeStruct((batch, dil, seq // dil, 3 * ATT_GROUP_W), BF16) for dil in ATT_DILATIONS]
    out_shape += [jax.ShapeDtypeStruct((n, 2 * ML_QK_W), F32),
                  jax.ShapeDtypeStruct((n, ML_V_W), BF16),
                  jax.ShapeDtypeStruct((n, ML_V_W), BF16),
                  jax.ShapeDtypeStruct((n, d_merge), BF16),
                  jax.ShapeDtypeStruct((n, GATE_PAD), F32)]
    out_specs = [att_spec(g, dil) for g, dil in enumerate(ATT_DILATIONS)]
    out_specs += [col_spec(9, 2), col_spec(11, 2), col_spec(13, 2), col_spec(15, d_merge // PROJ_TN),
                  pl.BlockSpec((tm, GATE_PAD), lambda i, j: (i, 0))]
    return pl.pallas_call(
        functools.partial(_inproj_kernel, tm=tm),
        out_shape=out_shape,
        grid=(n // tm, nj),
        in_specs=[pl.BlockSpec((tm, d), lambda i, j: (i, 0), pipeline_mode=pl.Buffered(1)),
                  pl.BlockSpec((1, 1, d), lambda i, j: (i // tpb, 0, 0)),
                  pl.BlockSpec((1, 1, d), lambda i, j: (i // tpb, 0, 0)),
                  pl.BlockSpec((1, d), lambda i, j: (0, 0)),
                  pl.BlockSpec((tm, HEAD_DIM), lambda i, j: (i % tpb, 0)),
                  pl.BlockSpec((tm, HEAD_DIM), lambda i, j: (i % tpb, 0)),
                  pl.BlockSpec((d, PROJ_TN), lambda i, j: (0, j)),
                  pl.BlockSpec((d, GATE_PAD), lambda i, j: (0, 0))],
        out_specs=out_specs,
        scratch_shapes=[pltpu.VMEM((tm, d), BF16), pltpu.VMEM((ATT_HEADS_PER_GROUP, tm, HEAD_DIM), F32)],
        compiler_params=_cparams(("parallel", "arbitrary")),
    )(x2d, scale, shift, norm_w, cos_t, sin_t, w_main, w_gates)


def _attention_kernel(*refs, pos_tile):
    groups = [refs[7 * g:7 * g + 7] for g in range(3)]
    o_ref = refs[21]
    kcat, vcat, acc_n, m_n, l_n = refs[22:27]
    i = pl.program_id(2)
    last = pl.num_programs(2) - 1
    sub = ML_CHUNK
    row = lax.broadcasted_iota(I32, (sub, 2 * sub), 0)
    col = lax.broadcasted_iota(I32, (sub, 2 * sub), 1)
    band = (col >= row) & (col - row <= 2 * ATT_HALF)
    for g, dil in enumerate(ATT_DILATIONS):
        q_ref, kp_ref, kc_ref, kn_ref, vp_ref, vc_ref, vn_ref = groups[g]
        r = pos_tile // dil
        for rho in range(dil):
            kcat[0:ATT_HALF] = kp_ref[0, rho]
            kcat[ATT_HALF:ATT_HALF + r] = kc_ref[0, rho]
            kcat[ATT_HALF + r:2 * ATT_HALF + r] = kn_ref[0, rho]
            vcat[0:ATT_HALF] = vp_ref[0, rho]
            vcat[ATT_HALF:ATT_HALF + r] = vc_ref[0, rho]
            vcat[ATT_HALF + r:2 * ATT_HALF + r] = vn_ref[0, rho]
            for s in range(r // sub):
                q = q_ref[0, rho, s * sub:(s + 1) * sub, :]
                k = kcat[s * sub:(s + 2) * sub, :]
                v = vcat[s * sub:(s + 2) * sub, :]
                sc = lax.dot_general(q, k, (((1,), (1,)), ((), ())), preferred_element_type=F32)
                kidx = col + s * sub
                ok = band & ((kidx >= ATT_HALF) | (i > 0)) & ((kidx < r + ATT_HALF) | (i < last))
                sc = jnp.where(ok, sc, MASK_VALUE)
                m = jnp.max(sc, axis=-1, keepdims=True)
                p = jnp.exp(sc - m)
                l = jnp.sum(p, axis=-1, keepdims=True)
                acc = jnp.dot(p.astype(BF16), v, preferred_element_type=F32)
                start = rho + dil * sub * s
                if dil == 1:
                    dst = pl.ds(start, sub)
                else:
                    dst = pl.ds(start, sub, stride=dil)
                acc_n[g, dst, :] = acc
                m_n[g, dst, :] = jnp.broadcast_to(m, (sub, HEAD_DIM))
                l_n[g, dst, :] = jnp.broadcast_to(l, (sub, HEAD_DIM))
    m_all = jnp.maximum(jnp.maximum(m_n[0], m_n[1]), m_n[2])
    num = jnp.zeros_like(m_all)
    den = jnp.zeros_like(m_all)
    for g in range(3):
        w = jnp.exp(m_n[g] - m_all)
        num = num + w * acc_n[g]
        den = den + w * l_n[g]
    o_ref[...] = (num / den).astype(BF16)


def attention_branch(att_groups, *, batch, seq):
    p = min(ATT_POS_TILE, seq)
    assert seq % p == 0 and p % (ML_CHUNK * ATT_DILATIONS[-1]) == 0
    nblk = seq // p
    hb = ATT_HALF
    in_specs = []
    args = []
    for g, dil in enumerate(ATT_DILATIONS):
        r = p // dil
        s_len = seq // dil
        n_half_blocks = s_len // hb

        def cur(col0, r=r, dil=dil):
            return pl.BlockSpec((1, dil, r, HEAD_DIM), lambda b, h, i: (b, 0, i, col0 + h))

        def prev(col0, r=r, dil=dil):
            return pl.BlockSpec((1, dil, hb, HEAD_DIM),
                                lambda b, h, i: (b, 0, jnp.maximum(i * (r // hb) - 1, 0), col0 + h))

        def nxt(col0, r=r, dil=dil, nhb=n_half_blocks):
            return pl.BlockSpec((1, dil, hb, HEAD_DIM),
                                lambda b, h, i: (b, 0, jnp.minimum((i + 1) * (r // hb), nhb - 1), col0 + h))

        kc0, vc0 = ATT_HEADS_PER_GROUP, 2 * ATT_HEADS_PER_GROUP
        in_specs += [cur(0), prev(kc0), cur(kc0), nxt(kc0), prev(vc0), cur(vc0), nxt(vc0)]
        args += [att_groups[g]] * 7
    return pl.pallas_call(
        functools.partial(_attention_kernel, pos_tile=p),
        out_shape=jax.ShapeDtypeStruct((batch * seq, ATT_GROUP_W), BF16),
        grid=(batch, ATT_HEADS_PER_GROUP, nblk),
        in_specs=in_specs,
        out_specs=pl.BlockSpec((p, HEAD_DIM), lambda b, h, i: (b * nblk + i, h)),
        scratch_shapes=[pltpu.VMEM((p + 2 * hb, HEAD_DIM), BF16),
                        pltpu.VMEM((p + 2 * hb, HEAD_DIM), BF16),
                        pltpu.VMEM((3, p, HEAD_DIM), F32),
                        pltpu.VMEM((3, p, HEAD_DIM), F32),
                        pltpu.VMEM((3, p, HEAD_DIM), F32)],
        compiler_params=_cparams(("parallel", "parallel", "arbitrary")),
    )(*args)


def _conv_kernel(prev_ref, cur_ref, next_ref, w_ref, b_ref, o_ref, xcat, *, tt, tpb):
    i = pl.program_id(0)
    first = (i % tpb) == 0
    last = (i % tpb) == tpb - 1
    xcat[0:8] = jnp.where(first, 0.0, prev_ref[...])
    xcat[8:8 + tt] = cur_ref[...]
    xcat[8 + tt:16 + tt] = jnp.where(last, 0.0, next_ref[...])
    acc = jnp.broadcast_to(b_ref[...], (tt, b_ref.shape[1]))
    for j in range(ML_CONV_TAPS):
        acc = acc + w_ref[j:j + 1, :] * xcat[pl.ds(8 - ML_CONV_TAPS // 2 + j, tt), :]
    y = _silu(acc)
    lane = lax.broadcasted_iota(I32, y.shape, 1)
    o_ref[...] = jnp.where(lane < ML_QK_W, y * QK_SCALE, y).astype(BF16)


def mlstm_conv(qk_raw, conv_w, conv_b, *, seq):
    n, c = qk_raw.shape
    tt = min(512, seq)
    tpb = seq // tt
    w_pad = jnp.zeros((8, c), F32).at[:ML_CONV_TAPS].set(conv_w)
    return pl.pallas_call(
        functools.partial(_conv_kernel, tt=tt, tpb=tpb),
        out_shape=jax.ShapeDtypeStruct((n, c), BF16),
        grid=(n // tt,),
        in_specs=[pl.BlockSpec((8, c), lambda i: (jnp.maximum(i * (tt // 8) - 1, 0), 0)),
                  pl.BlockSpec((tt, c), lambda i: (i, 0)),
                  pl.BlockSpec((8, c), lambda i: (jnp.minimum((i + 1) * (tt // 8), n // 8 - 1), 0)),
                  pl.BlockSpec((8, c), lambda i: (0, 0)),
                  pl.BlockSpec((1, c), lambda i: (0, 0))],
        out_specs=pl.BlockSpec((tt, c), lambda i: (i, 0)),
        scratch_shapes=[pltpu.VMEM((tt + 16, c), F32)],
        compiler_params=_cparams(("parallel",)),
    )(qk_raw, qk_raw, qk_raw, w_pad, conv_b.reshape(1, c))


def _log_sigmoid(x):
    return jnp.minimum(x, 0.0) - jnp.log(1.0 + jnp.exp(-jnp.abs(x)))


def _mlstm_kernel(q_ref, k_ref, v_ref, g_ref, bias_ref, h_ref, c_scr, n_scr, m_scr, *, reverse):
    cidx = pl.program_id(1)
    L = ML_CHUNK

    @pl.when(cidx == 0)
    def _():
        c_scr[...] = jnp.zeros_like(c_scr)
        n_scr[...] = jnp.zeros_like(n_scr)
        m_scr[...] = jnp.zeros_like(m_scr)

    row = lax.broadcasted_iota(I32, (L, L), 0)
    col = lax.broadcasted_iota(I32, (L, L), 1)
    causal = (col >= row) if reverse else (col <= row)
    gates = g_ref[...] + bias_ref[...]
    logf = _log_sigmoid(gates)
    b_cols = jnp.dot(causal.astype(F32), logf, preferred_element_type=F32,
                     precision=lax.Precision.HIGHEST)
    gates_t = gates.T
    b_rows = b_cols.T
    edge = 0 if reverse else L - 1
    off = 2 * ML_HEADS if reverse else 0
    for h in range(ML_HEADS):
        ci = off + h
        cf = off + ML_HEADS + h
        q = q_ref[:, h * HEAD_DIM:(h + 1) * HEAD_DIM]
        k = k_ref[:, h * HEAD_DIM:(h + 1) * HEAD_DIM]
        v = v_ref[:, h * ML_V_DIM:(h + 1) * ML_V_DIM]
        ib_r = gates_t[ci:ci + 1, :]
        ib_c = gates[:, ci:ci + 1]
        b_r = b_rows[cf:cf + 1, :]
        b_c = b_cols[:, cf:cf + 1]
        b_last = b_cols[edge:edge + 1, cf:cf + 1]
        m_prev = m_scr[h][:, 0:1]
        c_prev = c_scr[h]
        n_prev = n_scr[h]
        dlog = jnp.where(causal, b_c - b_r + ib_r, MASK_VALUE)
        inter = b_c + m_prev
        m_i = jnp.maximum(inter, jnp.max(dlog, axis=-1, keepdims=True))
        w = jnp.exp(dlog - m_i)
        s = lax.dot_general(q, k, (((1,), (1,)), ((), ())), preferred_element_type=F32) * w
        sc = jnp.exp(inter - m_i)
        num = jnp.dot(s.astype(BF16), v, preferred_element_type=F32)
        num = num + sc * jnp.dot(q, c_prev.astype(BF16), preferred_element_type=F32)
        qn = jnp.sum(q.astype(F32) * n_prev, axis=-1, keepdims=True)
        den = jnp.sum(s, axis=-1, keepdims=True) + sc * qn
        h_ref[:, h * ML_V_DIM:(h + 1) * ML_V_DIM] = num / jnp.maximum(jnp.abs(den), jnp.exp(-m_i))
        g_r = b_last - b_r + ib_r
        g_c = b_last - b_c + ib_c
        m_new = jnp.maximum(b_last + m_prev, jnp.max(g_r, axis=-1, keepdims=True))
        decay = jnp.exp(b_last + m_prev - m_new)
        kw = k.astype(F32) * jnp.exp(g_c - m_new)
        c_scr[h] = decay * c_prev + lax.dot_general(kw.astype(BF16), v, (((0,), (0,)), ((), ())),
                                                    preferred_element_type=F32)
        n_scr[h] = decay * n_prev + jnp.sum(kw, axis=0, keepdims=True)
        m_scr[h] = jnp.broadcast_to(m_new, (1, LANES))


def mlstm_scan(qk, v, gates, bias_pad, *, batch, seq, reverse):
    n = batch * seq
    nc = seq // ML_CHUNK

    def chunk(b, c):
        return b * nc + ((nc - 1 - c) if reverse else c)

    return pl.pallas_call(
        functools.partial(_mlstm_kernel, reverse=reverse),
        out_shape=jax.ShapeDtypeStruct((n, ML_V_W), F32),
        grid=(batch, nc),
        in_specs=[pl.BlockSpec((ML_CHUNK, ML_QK_W), lambda b, c: (chunk(b, c), 0)),
                  pl.BlockSpec((ML_CHUNK, ML_QK_W), lambda b, c: (chunk(b, c), 1)),
                  pl.BlockSpec((ML_CHUNK, ML_V_W), lambda b, c: (chunk(b, c), 0)),
                  pl.BlockSpec((ML_CHUNK, GATE_PAD), lambda b, c: (chunk(b, c), 0)),
                  pl.BlockSpec((1, GATE_PAD), lambda b, c: (0, 0))],
        out_specs=pl.BlockSpec((ML_CHUNK, ML_V_W), lambda b, c: (chunk(b, c), 0)),
        scratch_shapes=[pltpu.VMEM((ML_HEADS, HEAD_DIM, ML_V_DIM), F32),
                        pltpu.VMEM((ML_HEADS, 1, HEAD_DIM), F32),
                        pltpu.VMEM((ML_HEADS, 1, LANES), F32)],
        compiler_params=_cparams(("parallel", "arbitrary")),
    )(qk, qk, v, gates, bias_pad)


def _postmix_kernel(att_ref, hf_ref, hb_ref, mlo_ref, mg_ref, x_ref, g1_ref, sc2_ref, sh2_ref,
                    n2w_ref, mlnw_ref, wml_ref, watt_ref, wo_ref, wr_ref,
                    x1_ref, h2_ref, lg_ref, *, d):
    hm = hf_ref[...] + hb_ref[...]
    parts = []
    for h in range(ML_HEADS):
        hh = hm[:, h * ML_V_DIM:(h + 1) * ML_V_DIM]
        parts.append(hh * lax.rsqrt(jnp.mean(hh * hh, axis=-1, keepdims=True) + NORM_EPS))
    hn = jnp.concatenate(parts, axis=1) * mlnw_ref[...]
    ml_act = (_sigmoid(mlo_ref[...].astype(F32)) * hn).astype(BF16)
    y_ml = jnp.dot(ml_act, wml_ref[...], preferred_element_type=F32)
    y_att = jnp.dot(att_ref[...], watt_ref[...], preferred_element_type=F32)
    g_ml = _sigmoid(mg_ref[:, 0:d].astype(F32))
    g_att = _sigmoid(mg_ref[:, d:2 * d].astype(F32))
    merged = (g_ml * y_ml + g_att * y_att).astype(BF16)
    out = jnp.dot(merged, wo_ref[...], preferred_element_type=F32)
    x1 = x_ref[...] + g1_ref[0] * out
    x1_ref[...] = x1
    ms = jnp.mean(x1 * x1, axis=-1, keepdims=True)
    h2 = x1 * lax.rsqrt(ms + NORM_EPS) * n2w_ref[...]
    h2 = h2 * (1.0 + sc2_ref[0]) + sh2_ref[0]
    h2_ref[...] = _pack_pair(h2[:, :d // 2], h2[:, d // 2:])
    lg_ref[...] = jnp.dot(h2, wr_ref[...], preferred_element_type=F32, precision=lax.Precision.HIGHEST)


def post_mixer(att_o, h_f, h_b, ml_o, merge, x2d, gate1, scale2, shift2, norm2_w, ml_norm_w,
               w_ml_out, w_att_out, w_o, w_router_pad, *, seq):
    n, d = x2d.shape
    tm = 256
    tpb = seq // tm
    const = lambda shape: pl.BlockSpec(shape, lambda i: tuple(0 for _ in shape), pipeline_mode=pl.Buffered(1))
    rows = lambda w: pl.BlockSpec((tm, w), lambda i: (i, 0))
    per_b = pl.BlockSpec((1, 1, d), lambda i: (i // tpb, 0, 0))
    return pl.pallas_call(
        functools.partial(_postmix_kernel, d=d),
        out_shape=[jax.ShapeDtypeStruct((n, d), F32),
                   jax.ShapeDtypeStruct((n, d // 2), I32),
                   jax.ShapeDtypeStruct((n, LANES), F32)],
        grid=(n // tm,),
        in_specs=[rows(ATT_GROUP_W), rows(ML_V_W), rows(ML_V_W), rows(ML_V_W), rows(2 * d), rows(d),
                  per_b, per_b, per_b, const((1, d)), const((1, ML_V_W)),
                  const((ML_V_W, d)), const((ATT_GROUP_W, d)), const((d, d)), const((d, LANES))],
        out_specs=[rows(d), rows(d // 2), rows(LANES)],
        compiler_params=_cparams(("parallel",)),
    )(att_o, h_f, h_b, ml_o, merge, x2d, gate1, scale2, shift2, norm2_w, ml_norm_w,
      w_ml_out, w_att_out, w_o, w_router_pad)


def _split3(x):
    p0 = x.astype(BF16)
    r1 = x - p0.astype(F32)
    p1 = r1.astype(BF16)
    p2 = (r1 - p1.astype(F32)).astype(BF16)
    return p0, p1, p2


def _route_select_kernel(lg_ref, gate_ref, dest_ref, lcs_ref, bp_ref, bs_ref, off_ref, cnt_ref, *, cap, n_tok):
    lg = lg_ref[...]
    e, nb, _ = lg.shape
    mx = jnp.max(lg, axis=0, keepdims=True)
    ex = jnp.exp(lg - mx)
    aff = ex / jnp.sum(ex, axis=0, keepdims=True)
    bits = lax.bitcast_convert_type(aff, I32)

    def count(pred):
        c = jnp.sum(jnp.where(pred, 1.0, 0.0), axis=1, keepdims=True)
        return jnp.sum(c, axis=2, keepdims=True)

    capf = jnp.float32(cap)

    def thr_step(_, c):
        lo, hi = c
        mid = lo + lax.shift_right_logical(hi - lo + 1, 1)
        ok = count(bits >= mid) >= capf
        return jnp.where(ok, mid, lo), jnp.where(ok, hi, mid - 1)

    lo0 = jnp.zeros((e, 1, 1), I32)
    hi0 = jnp.full((e, 1, 1), 0x7F800000, I32)
    thr, _ = lax.fori_loop(0, 31, thr_step, (lo0, hi0))
    gt = bits > thr
    eq = bits == thr
    need = capf - count(gt)
    nidx = (lax.broadcasted_iota(I32, (1, nb, LANES), 1) * LANES + lax.broadcasted_iota(I32, (1, nb, LANES), 2))

    def tie_step(_, c):
        lo, hi = c
        mid = lax.shift_right_logical(lo + hi, 1)
        ok = count(eq & (nidx < mid)) >= need
        return jnp.where(ok, lo, mid), jnp.where(ok, mid, hi)

    n_iter = max(1, (n_tok - 1).bit_length() + 1)
    _, jcut = lax.fori_loop(0, n_iter, tie_step, (jnp.zeros((e, 1, 1), I32), jnp.full((e, 1, 1), n_tok, I32)))
    mask = jnp.where(gt | (eq & (nidx < jcut)), 1.0, 0.0)
    gate_ref[...] = aff * mask

    jr = lax.broadcasted_iota(I32, (LANES, LANES), 0)
    jc = lax.broadcasted_iota(I32, (LANES, LANES), 1)
    tri_incl = jnp.where(jr <= jc, 1.0, 0.0).astype(BF16)
    br = lax.broadcasted_iota(I32, (nb, nb), 0)
    bc = lax.broadcasted_iota(I32, (nb, nb), 1)
    tri_prev = jnp.where(bc < br, 1.0, 0.0).astype(BF16)

    lcs = jnp.dot(mask.reshape(e * nb, LANES).astype(BF16), tri_incl, preferred_element_type=F32)
    lcs = lcs.reshape(e, nb, LANES)
    lcs_ref[...] = lcs.astype(BF16)
    cnt = jnp.zeros((nb, LANES), F32)
    ranks = []
    for ei in range(e):
        bs = jnp.broadcast_to(lcs[ei][:, LANES - 1:LANES], (nb, LANES))
        bs_ref[ei] = bs
        bp_ref[ei] = jnp.dot(tri_prev, bs.astype(BF16), preferred_element_type=F32)
        ranks.append(cnt)
        cnt = cnt + mask[ei]
    lcs_k = jnp.dot(cnt.astype(BF16), tri_incl, preferred_element_type=F32)
    bs_k = jnp.broadcast_to(lcs_k[:, LANES - 1:LANES], (nb, LANES))
    bs_hi = jnp.floor(bs_k * (1.0 / 64.0))
    bs_lo = bs_k - 64.0 * bs_hi
    bp_k = (64.0 * jnp.dot(tri_prev, bs_hi.astype(BF16), preferred_element_type=F32)
            + jnp.dot(tri_prev, bs_lo.astype(BF16), preferred_element_type=F32))
    off = bp_k + lcs_k - cnt
    off_ref[...] = off.astype(I32)
    cnt_ref[...] = cnt.astype(I32)
    for ei in range(e):
        dest_ref[ei] = jnp.where(mask[ei] > 0.0, off + ranks[ei], -1.0)


def _route_compact_kernel(lcs_ref, bp_ref, bs_ref, dest_ref, idx_out, dst_out, *, cap):
    lcs = lcs_ref[0]
    bp = bp_ref[0]
    bs = bs_ref[0]
    nb = lcs.shape[0]
    blk_id = lax.broadcasted_iota(I32, (nb, LANES), 0).astype(F32)
    lane = lax.broadcasted_iota(I32, (1, LANES), 1).astype(F32)
    oh_parts, base_parts, blk_parts = [], [], []
    for k in range(cap // LANES):
        r = lane + float(k * LANES)
        hit = (bp <= r) & (r < bp + bs)
        oh_parts.append(jnp.where(hit, 1.0, 0.0).astype(BF16))
        base_parts.append(jnp.sum(jnp.where(hit, bp, 0.0), axis=0, keepdims=True))
        blk_parts.append(jnp.sum(jnp.where(hit, blk_id, 0.0), axis=0, keepdims=True))
    onehot = jnp.concatenate(oh_parts, axis=1)
    base = jnp.concatenate(base_parts, axis=1)
    blk = jnp.concatenate(blk_parts, axis=1)
    rows = lax.broadcasted_iota(I32, (1, cap), 1).astype(F32)
    contract0 = (((0,), (0,)), ((), ()))
    counts = lax.dot_general(lcs, onehot, contract0, preferred_element_type=F32)
    jloc = jnp.sum(jnp.where(counts <= rows - base, 1.0, 0.0), axis=0, keepdims=True)
    dvals = jnp.zeros((LANES, cap), F32)
    for piece in _split3(dest_ref[0]):
        dvals = dvals + lax.dot_general(piece, onehot, contract0, preferred_element_type=F32)
    jcol = lax.broadcasted_iota(I32, (LANES, cap), 0).astype(F32)
    dst = jnp.sum(jnp.where(jcol == jloc, dvals, 0.0), axis=0, keepdims=True)
    idx_out[0] = (blk * float(LANES) + jloc).astype(I32)
    dst_out[0] = dst.astype(I32)


def route_tokens(logits, n_experts):
    n = logits.shape[0]
    e = n_experts
    nb = n // LANES
    cap = CAPACITY_FACTOR * n // e
    assert n % LANES == 0 and cap % LANES == 0
    lg = logits[:, :e].T.reshape(e, nb, LANES)
    big = lambda dt: jax.ShapeDtypeStruct((e, nb, LANES), dt)
    small = jax.ShapeDtypeStruct((nb, LANES), I32)
    gate, dest, lcs, bp, bs, off, cnt = pl.pallas_call(
        functools.partial(_route_select_kernel, cap=cap, n_tok=n),
        out_shape=[big(F32), big(F32), big(BF16), big(F32), big(F32), small, small],
        compiler_params=pltpu.CompilerParams(vmem_limit_bytes=VMEM_LIMIT),
    )(lg)
    per_e = lambda: pl.BlockSpec((1, nb, LANES), lambda i: (i, 0, 0))
    idx, dst = pl.pallas_call(
        functools.partial(_route_compact_kernel, cap=cap),
        out_shape=[jax.ShapeDtypeStruct((e, 1, cap), I32), jax.ShapeDtypeStruct((e, 1, cap), I32)],
        grid=(e,),
        in_specs=[per_e(), per_e(), per_e(), per_e()],
        out_specs=[pl.BlockSpec((1, 1, cap), lambda i: (i, 0, 0)), pl.BlockSpec((1, 1, cap), lambda i: (i, 0, 0))],
        compiler_params=_cparams(("parallel",)),
    )(lcs, bp, bs, dest)
    gate_tok = gate.reshape(e, n).T
    dest_tok = dest.reshape(e, n).T.astype(I32)
    return idx.reshape(-1), dst.reshape(-1), gate_tok, dest_tok, off.reshape(-1), cnt.reshape(-1)


def _expert_kernel(idx_ref, idxn_ref, dst_ref, wg_ref, wu_ref, wd_ref, h2_hbm, z_hbm,
                   xbuf, zbuf, xs, acc, gsem, ssem, *, d, ff, tr):
    s = pl.program_id(0)
    nsteps = pl.num_programs(0)
    half = d // 2

    def start_gather(ids):
        for r in range(tr):
            pltpu.make_async_copy(h2_hbm.at[pl.ds(ids[0, 0, r], 1), :], xbuf.at[pl.ds(r, 1), :], gsem.at[0]).start()

    def wait_gather():
        pltpu.make_async_copy(h2_hbm.at[pl.ds(0, tr), :], xbuf, gsem.at[0]).wait()

    def wait_scatter():
        pltpu.make_async_copy(zbuf, z_hbm.at[pl.ds(0, tr), :], ssem.at[0]).wait()

    @pl.when(s == 0)
    def _():
        start_gather(idx_ref)

    wait_gather()
    lo, hi = _unpack_pair(xbuf[...])
    xs[:, :half] = lo.astype(BF16)
    xs[:, half:] = hi.astype(BF16)
    start_gather(idxn_ref)

    x = xs[...]
    tf = min(ff, 512)
    for f in range(ff // tf):
        g = jnp.dot(x, wg_ref[0, :, f * tf:(f + 1) * tf], preferred_element_type=F32)
        u = jnp.dot(x, wu_ref[0, :, f * tf:(f + 1) * tf], preferred_element_type=F32)
        hid = (_silu(g) * u).astype(BF16)
        part = jnp.dot(hid, wd_ref[0, f * tf:(f + 1) * tf, :], preferred_element_type=F32)
        if f == 0:
            acc[...] = part
        else:
            acc[...] += part

    @pl.when(s > 0)
    def _():
        wait_scatter()

    zbuf[...] = _pack_pair(acc[:, :half], acc[:, half:])
    for r in range(tr):
        pltpu.make_async_copy(zbuf.at[pl.ds(r, 1), :], z_hbm.at[pl.ds(dst_ref[0, 0, r], 1), :], ssem.at[0]).start()

    @pl.when(s == nsteps - 1)
    def _():
        wait_scatter()
        wait_gather()


def expert_ffn(h2_packed, idx, dest, w_gate, w_up, w_down):
    n, half = h2_packed.shape
    e, d, ff = w_gate.shape
    total = idx.shape[0]
    cap = total // e
    tr = min(EXPERT_ROWS, cap)
    nsteps = total // tr
    per_e = cap // tr
    idx3 = idx.reshape(nsteps, 1, tr)
    dst3 = dest.reshape(nsteps, 1, tr)
    smem = lambda f: pl.BlockSpec((1, 1, tr), f, memory_space=pltpu.SMEM)
    wspec = lambda shape: pl.BlockSpec(shape, lambda s: (s // per_e, 0, 0), pipeline_mode=pl.Buffered(1))
    return pl.pallas_call(
        functools.partial(_expert_kernel, d=d, ff=ff, tr=tr),
        out_shape=jax.ShapeDtypeStruct((total, half), I32),
        grid=(nsteps,),
        in_specs=[smem(lambda s: (s, 0, 0)),
                  smem(lambda s: (jnp.minimum(s + 1, nsteps - 1), 0, 0)),
                  smem(lambda s: (s, 0, 0)),
                  wspec((1, d, ff)), wspec((1, d, ff)), wspec((1, ff, d)),
                  pl.BlockSpec(memory_space=pl.ANY)],
        out_specs=pl.BlockSpec(memory_space=pl.ANY),
        scratch_shapes=[pltpu.VMEM((tr, half), I32),
                        pltpu.VMEM((tr, half), I32),
                        pltpu.VMEM((tr, d), BF16),
                        pltpu.VMEM((tr, d), F32),
                        pltpu.SemaphoreType.DMA((1,)),
                        pltpu.SemaphoreType.DMA((1,))],
        compiler_params=_cparams(("arbitrary",)),
    )(idx3, idx3, dst3, w_gate, w_up, w_down, h2_packed)


def _combine_kernel(tile_ref, chunk_ref, flag_ref, z_ref, gt_ref, dt_ref, x_ref, g2_ref, fw_ref, o_ref,
                    acc, *, d, tm, tr, n_experts, final):
    w = pl.program_id(0)
    flags = flag_ref[w]

    @pl.when((flags & 2) != 0)
    def _():
        acc[...] = jnp.zeros_like(acc)

    @pl.when((flags & 1) != 0)
    def _():
        lo, hi = _unpack_pair(z_ref[...])
        zs = jnp.concatenate([lo.astype(BF16), hi.astype(BF16)], axis=1)
        zrow = lax.broadcasted_iota(I32, (tm, tr), 1) + chunk_ref[w] * tr
        gate = gt_ref[...]
        dest = dt_ref[...]
        sel = jnp.zeros((tm, tr), F32)
        for ei in range(n_experts):
            sel = sel + jnp.where(zrow == dest[:, ei:ei + 1], gate[:, ei:ei + 1], 0.0)
        sel_hi = sel.astype(BF16)
        sel_lo = (sel - sel_hi.astype(F32)).astype(BF16)
        acc[...] += (jnp.dot(sel_hi, zs, preferred_element_type=F32)
                     + jnp.dot(sel_lo, zs, preferred_element_type=F32))

    @pl.when((flags & 4) != 0)
    def _():
        x2 = x_ref[...] + g2_ref[0] * acc[...]
        if final:
            ms = jnp.mean(x2 * x2, axis=-1, keepdims=True)
            x2 = x2 * lax.rsqrt(ms + NORM_EPS) * fw_ref[...]
        o_ref[...] = x2


def moe_combine(z_sorted, item_tile, item_chunk, item_flags, gate_tok, dest_tok, x1, gate2, final_w, *, seq, final):
    n, d = x1.shape
    total, half = z_sorted.shape
    e = gate_tok.shape[1]
    tm = min(MOE_ROWS, n)
    tr = min(MOE_ROWS, total)
    tpb = seq // tm
    n_items = item_tile.shape[0]
    return pl.pallas_call(
        functools.partial(_combine_kernel, d=d, tm=tm, tr=tr, n_experts=e, final=final),
        out_shape=jax.ShapeDtypeStruct((n, d), F32),
        grid_spec=pltpu.PrefetchScalarGridSpec(
            num_scalar_prefetch=3,
            grid=(n_items,),
            in_specs=[pl.BlockSpec((tr, half), lambda w, t, c, f: (c[w], 0)),
                      pl.BlockSpec((tm, e), lambda w, t, c, f: (t[w], 0)),
                      pl.BlockSpec((tm, e), lambda w, t, c, f: (t[w], 0)),
                      pl.BlockSpec((tm, d), lambda w, t, c, f: (t[w], 0)),
                      pl.BlockSpec((1, 1, d), lambda w, t, c, f: (t[w] // tpb, 0, 0)),
                      pl.BlockSpec((1, d), lambda w, t, c, f: (0, 0))],
            out_specs=pl.BlockSpec((tm, d), lambda w, t, c, f: (t[w], 0)),
            scratch_shapes=[pltpu.VMEM((tm, d), F32)]),
        compiler_params=_cparams(("arbitrary",)),
    )(item_tile, item_chunk, item_flags, z_sorted, gate_tok, dest_tok, x1, gate2, final_w)


def _combine_items(off, n, total, tm, tr):
    nt = n // tm
    nchunks = total // tr
    first = off.reshape(nt, tm)[:, 0]
    end = jnp.concatenate([first[1:], jnp.array([total], I32)])
    c_lo = jnp.minimum(first // tr, nchunks - 1)
    c_hi = jnp.where(end > first, (end - 1) // tr, c_lo)
    cnt = c_hi - c_lo + 1
    start = jnp.cumsum(cnt) - cnt
    n_items = nt + nchunks
    w = jnp.arange(n_items, dtype=I32)
    t = jnp.clip(jnp.searchsorted(start, w, side="right").astype(I32) - 1, 0, nt - 1)
    k = w - start[t]
    valid = k < cnt[t]
    k = jnp.minimum(k, cnt[t] - 1)
    flags = jnp.where(valid, 1 + 2 * (k == 0) + 4 * (k == cnt[t] - 1), 0)
    return t.astype(I32), (c_lo[t] + k).astype(I32), flags.astype(I32)


def _rope_tables(seq):
    freqs = jnp.float32(ROT_THETA) ** (-jnp.arange(ROT_HALF, dtype=F32) / ROT_HALF)
    ang = jnp.arange(seq, dtype=F32)[:, None] * freqs[None, :]
    cos = jnp.cos(ang)
    sin = jnp.sin(ang)
    ones = jnp.ones((seq, HEAD_DIM - 2 * ROT_HALF), F32)
    cos_t = jnp.concatenate([cos, cos, ones], axis=1)
    sin_t = jnp.concatenate([-sin, sin, 0.0 * ones], axis=1)
    return cos_t, sin_t


def _prep_layer(p, l):
    w_in = p["w_in"][l]
    aw = 3 * ATT_GROUP_W
    q, k, v = w_in[:, 0:aw], w_in[:, aw:2 * aw], w_in[:, 2 * aw:3 * aw]
    cols = []
    for g in range(3):
        sl = slice(g * ATT_GROUP_W, (g + 1) * ATT_GROUP_W)
        cols += [q[:, sl], k[:, sl], v[:, sl]]
    o = 3 * aw
    cols.append(w_in[:, o:o + 2 * ML_QK_W + 2 * ML_V_W])
    o += 2 * ML_QK_W + 2 * ML_V_W
    w_gates = w_in[:, o:o + 4 * ML_HEADS]
    o += 4 * ML_HEADS
    cols.append(w_in[:, o:])
    d = w_in.shape[0]
    return dict(
        w_main=jnp.concatenate(cols, axis=1).astype(BF16),
        w_gates=jnp.zeros((d, GATE_PAD), BF16).at[:, :4 * ML_HEADS].set(w_gates.astype(BF16)),
        bias_pad=jnp.zeros((1, GATE_PAD), F32).at[0, :4 * ML_HEADS].set(p["b_mgate"][l]),
        w_ml_out=p["w_ml_out"][l].astype(BF16),
        w_att_out=p["w_att_out"][l].astype(BF16),
        w_o=p["w_o"][l].astype(BF16),
        w_router=jnp.zeros((d, LANES), F32).at[:, :N_EXPERTS].set(p["w_router"][l]),
        w_exp_gate=p["w_exp_gate"][l].astype(BF16),
        w_exp_up=p["w_exp_up"][l].astype(BF16),
        w_exp_down=p["w_exp_down"][l].astype(BF16),
    )


def _moe_sublayer(x1, h2p, logits, gate2, lw, final_w, *, seq, final):
    n = x1.shape[0]
    idx, dest, gate_tok, dest_tok, off, _ = route_tokens(logits, N_EXPERTS)
    z = expert_ffn(h2p, idx, dest, lw["w_exp_gate"], lw["w_exp_up"], lw["w_exp_down"])
    tm = min(MOE_ROWS, n)
    tr = min(MOE_ROWS, z.shape[0])
    items = _combine_items(off, n, z.shape[0], tm, tr)
    return moe_combine(z, *items, gate_tok, dest_tok, x1, gate2, final_w, seq=seq, final=final)


def _run_group(x, mods, layers, p, cos_t, sin_t):
    batch, seq, d = x.shape
    n = batch * seq
    x2d = x.reshape(n, d)
    depth = len(layers)
    for l, lw in enumerate(layers):
        shift1, scale1, gate1, shift2, scale2, gate2 = [m.reshape(batch, 1, d) for m in jnp.split(mods[l], 6, axis=-1)]
        a0, a1, a2, ml_qk, ml_v, ml_o, merge, gates = input_projection(
            x2d, scale1, shift1, p["norm1_w"][l].reshape(1, d), cos_t, sin_t, lw["w_main"], lw["w_gates"],
            batch=batch, seq=seq)
        att_o = attention_branch((a0, a1, a2), batch=batch, seq=seq)
        qk = mlstm_conv(ml_qk, p["conv_w"][l], p["conv_b"][l], seq=seq)
        h_f = mlstm_scan(qk, ml_v, gates, lw["bias_pad"], batch=batch, seq=seq, reverse=False)
        h_b = mlstm_scan(qk, ml_v, gates, lw["bias_pad"], batch=batch, seq=seq, reverse=True)
        x1, h2p, logits = post_mixer(att_o, h_f, h_b, ml_o, merge, x2d, gate1, scale2, shift2,
                                     p["norm2_w"][l].reshape(1, d), p["ml_norm_w"][l].reshape(1, ML_V_W),
                                     lw["w_ml_out"], lw["w_att_out"], lw["w_o"], lw["w_router"], seq=seq)
        x2d = _moe_sublayer(x1, h2p, logits, gate2, lw, p["final_norm_w"].reshape(1, d),
                            seq=seq, final=(l == depth - 1))
    return x2d.reshape(batch, seq, d)


def kernel(x_prompt, x_sample, c_prompt, c_sample, norm1_w, norm2_w, w_ada, b_ada, w_in, b_mgate, conv_w, conv_b, ml_norm_w, w_ml_out, w_att_out, w_o, w_router, w_exp_gate, w_exp_up, w_exp_down, final_norm_w):
    p = dict(norm1_w=norm1_w, norm2_w=norm2_w, w_in=w_in, b_mgate=b_mgate, conv_w=conv_w, conv_b=conv_b,
             ml_norm_w=ml_norm_w, w_ml_out=w_ml_out, w_att_out=w_att_out, w_o=w_o, w_router=w_router,
             w_exp_gate=w_exp_gate, w_exp_up=w_exp_up, w_exp_down=w_exp_down, final_norm_w=final_norm_w)
    depth = w_in.shape[0]
    bp, bs = c_prompt.shape[0], c_sample.shape[0]
    m_pad = -(-(bp + bs) // 8) * 8
    c_pad = jnp.zeros((m_pad, c_prompt.shape[1]), F32).at[:bp].set(c_prompt).at[bp:bp + bs].set(c_sample)
    mods = ada_modulation(c_pad, w_ada, b_ada)
    layers = [_prep_layer(p, l) for l in range(depth)]
    outs = []
    for x, rows in ((x_prompt, slice(0, bp)), (x_sample, slice(bp, bp + bs))):
        cos_t, sin_t = _rope_tables(x.shape[1])
        outs.append(_run_group(x, mods[:, rows], layers, p, cos_t, sin_t))
    return tuple(outs)
```

```python
import functools
import math

import jax
import jax.numpy as jnp
from jax import lax
from jax.experimental import pallas as pl
from jax.experimental.pallas import tpu as pltpu

F32 = jnp.float32
BF16 = jnp.bfloat16
I32 = jnp.int32

NORM_EPS = 1e-6
MASK_VALUE = -1e30
HEAD_DIM = 128
ATT_DILATIONS = (1, 4, 16)
ATT_WINDOWS = (128, 512, 2048)
ATT_HALF = 64
ATT_HEADS_PER_GROUP = 4
ATT_GROUP_W = ATT_HEADS_PER_GROUP * HEAD_DIM
ROT_HALF = 16
ROT_THETA = 500000.0
ML_HEADS = 4
ML_V_DIM = 256
ML_QK_W = ML_HEADS * HEAD_DIM
ML_V_W = ML_HEADS * ML_V_DIM
ML_CHUNK = 128
ML_CONV_TAPS = 5
N_EXPERTS = 16
CAPACITY_FACTOR = 2
GATE_PAD = 128
LANES = 128
VMEM_LIMIT = 56 * 1024 * 1024
QK_SCALE = HEAD_DIM ** -0.5

PROJ_TN = 512
ATT_POS_TILE = 2048
MOE_ROWS = 256
EXPERT_ROWS = 512


def _cparams(sem):
    return pltpu.CompilerParams(dimension_semantics=sem, vmem_limit_bytes=VMEM_LIMIT)


def _sigmoid(x):
    return 1.0 / (1.0 + jnp.exp(-x))


def _silu(x):
    return x * _sigmoid(x)


def _bf16_bits(x):
    b = lax.bitcast_convert_type(x, I32)
    return b + 0x7FFF + (lax.shift_right_logical(b, 16) & 1)


def _pack_pair(lo, hi):
    return (_bf16_bits(hi) & (-65536)) | lax.shift_right_logical(_bf16_bits(lo), 16)


def _unpack_pair(w):
    lo = lax.bitcast_convert_type(lax.shift_left(w, 16), F32)
    hi = lax.bitcast_convert_type(w & (-65536), F32)
    return lo, hi


def _ada_kernel(c_ref, w_ref, b_ref, o_ref):
    c = c_ref[...]
    a = _silu(c).astype(BF16)
    o_ref[0] = jnp.dot(a, w_ref[0].astype(BF16), preferred_element_type=F32) + b_ref[0]


def ada_modulation(c_pad, w_ada, b_ada):
    m, d = c_pad.shape
    depth, _, n = w_ada.shape
    tn = min(n, 1536)
    return pl.pallas_call(
        _ada_kernel,
        out_shape=jax.ShapeDtypeStruct((depth, m, n), F32),
        grid=(depth, n // tn),
        in_specs=[pl.BlockSpec((m, d), lambda l, j: (0, 0)),
                  pl.BlockSpec((1, d, tn), lambda l, j: (l, 0, j)),
                  pl.BlockSpec((1, 1, tn), lambda l, j: (l, 0, j))],
        out_specs=pl.BlockSpec((1, m, tn), lambda l, j: (l, 0, j)),
        compiler_params=_cparams(("parallel", "parallel")),
    )(c_pad, w_ada, b_ada.reshape(depth, 1, n))


def _rope(a, cos, sin_signed):
    up = pltpu.roll(a, HEAD_DIM - ROT_HALF, axis=1)
    dn = pltpu.roll(a, ROT_HALF, axis=1)
    lane = lax.broadcasted_iota(I32, a.shape, 1)
    partner = jnp.where(lane < ROT_HALF, up, dn)
    return a * cos + partner * sin_signed


def _inproj_kernel(x_ref, sc_ref, sh_ref, nw_ref, cos_ref, sin_ref, w_ref, wg_ref,
                   a0_ref, a1_ref, a2_ref, qk_ref, v_ref, o_ref, mg_ref, gt_ref,
                   h_scr, acc_scr, *, tm):
    j = pl.program_id(1)

    @pl.when(j == 0)
    def _():
        x = x_ref[...]
        ms = jnp.mean(x * x, axis=-1, keepdims=True)
        h = x * lax.rsqrt(ms + NORM_EPS) * nw_ref[...]
        h = h * (1.0 + sc_ref[0]) + sh_ref[0]
        hb = h.astype(BF16)
        h_scr[...] = hb
        gt_ref[...] = jnp.dot(hb, wg_ref[...], preferred_element_type=F32)

    def project():
        return jnp.dot(h_scr[...], w_ref[...], preferred_element_type=F32)

    att_refs = (a0_ref, a1_ref, a2_ref)
    for g, dil in enumerate(ATT_DILATIONS):
        for t in range(3):

            @pl.when(j == 3 * g + t)
            def _(g=g, dil=dil, t=t):
                a = project()
                if t < 2:
                    cos = cos_ref[...]
                    sin = sin_ref[...]
                    heads = []
                    for hh in range(ATT_HEADS_PER_GROUP):
                        ah = a[:, hh * HEAD_DIM:(hh + 1) * HEAD_DIM]
                        if t == 0:
                            ah = ah * QK_SCALE
                        heads.append(_rope(ah, cos, sin))
                    a = jnp.concatenate(heads, axis=1)
                out = att_refs[g]
                if dil == 1:
                    out[0, 0] = a.astype(BF16)
                else:
                    for hh in range(ATT_HEADS_PER_GROUP):
                        acc_scr[hh] = a[:, hh * HEAD_DIM:(hh + 1) * HEAD_DIM]
                    for rho in range(dil):
                        for hh in range(ATT_HEADS_PER_GROUP):
                            out[0, rho, :, hh * HEAD_DIM:(hh + 1) * HEAD_DIM] = (
                                acc_scr[hh, pl.ds(rho, tm // dil, stride=dil), :].astype(BF16))

    @pl.when((j == 9) | (j == 10))
    def _():
        qk_ref[...] = project()

    @pl.when((j == 11) | (j == 12))
    def _():
        v_ref[...] = project().astype(BF16)

    @pl.when((j == 13) | (j == 14))
    def _():
        o_ref[...] = project().astype(BF16)

    @pl.when(j >= 15)
    def _():
        mg_ref[...] = project().astype(BF16)


def input_projection(x2d, scale, shift, norm_w, cos_t, sin_t, w_main, w_gates, *, batch, seq):
    n, d = x2d.shape
    tm = min(1024, seq)
    tpb = seq // tm
    nj = w_main.shape[1] // PROJ_TN
    d_merge = w_main.shape[1] - (9 + 6) * PROJ_TN
    assert nj == 15 + d_merge // PROJ_TN and seq % tm == 0 and tm % (16 * 16) == 0

    def att_spec(g, dil):
        return pl.BlockSpec((1, dil, tm // dil, PROJ_TN),
                            lambda i, j: (i // tpb, 0, i % tpb, jnp.clip(j - 3 * g, 0, 2)))

    def col_spec(first, count):
        return pl.BlockSpec((tm, PROJ_TN), lambda i, j: (i, jnp.clip(j - first, 0, count - 1)))

    out_shape = [jax.ShapeDtypeStruct((batch, dil, seq // dil, 3 * ATT_GROUP_W), BF16) for dil in ATT_DILATIONS]
    out_shape += [jax.ShapeDtypeStruct((n, 2 * ML_QK_W), F32),
                  jax.ShapeDtypeStruct((n, ML_V_W), BF16),
                  jax.ShapeDtypeStruct((n, ML_V_W), BF16),
                  jax.ShapeDtypeStruct((n, d_merge), BF16),
                  jax.ShapeDtypeStruct((n, GATE_PAD), F32)]
    out_specs = [att_spec(g, dil) for g, dil in enumerate(ATT_DILATIONS)]
    out_specs += [col_spec(9, 2), col_spec(11, 2), col_spec(13, 2), col_spec(15, d_merge // PROJ_TN),
                  pl.BlockSpec((tm, GATE_PAD), lambda i, j: (i, 0))]
    return pl.pallas_call(
        functools.partial(_inproj_kernel, tm=tm),
        out_shape=out_shape,
        grid=(n // tm, nj),
        in_specs=[pl.BlockSpec((tm, d), lambda i, j: (i, 0), pipeline_mode=pl.Buffered(1)),
                  pl.BlockSpec((1, 1, d), lambda i, j: (i // tpb, 0, 0)),
                  pl.BlockSpec((1, 1, d), lambda i, j: (i // tpb, 0, 0)),
                  pl.BlockSpec((1, d), lambda i, j: (0, 0)),
                  pl.BlockSpec((tm, HEAD_DIM), lambda i, j: (i % tpb, 0)),
                  pl.BlockSpec((tm, HEAD_DIM), lambda i, j: (i % tpb, 0)),
                  pl.BlockSpec((d, PROJ_TN), lambda i, j: (0, j)),
                  pl.BlockSpec((d, GATE_PAD), lambda i, j: (0, 0))],
        out_specs=out_specs,
        scratch_shapes=[pltpu.VMEM((tm, d), BF16), pltpu.VMEM((ATT_HEADS_PER_GROUP, tm, HEAD_DIM), F32)],
        compiler_params=_cparams(("parallel", "arbitrary")),
    )(x2d, scale, shift, norm_w, cos_t, sin_t, w_main, w_gates)


def _attention_kernel(*refs, pos_tile):
    groups = [refs[7 * g:7 * g + 7] for g in range(3)]
    o_ref = refs[21]
    kcat, vcat, acc_n, m_n, l_n = refs[22:27]
    i = pl.program_id(2)
    last = pl.num_programs(2) - 1
    sub = ML_CHUNK
    row = lax.broadcasted_iota(I32, (sub, 2 * sub), 0)
    col = lax.broadcasted_iota(I32, (sub, 2 * sub), 1)
    band = (col >= row) & (col - row <= 2 * ATT_HALF)
    for g, dil in enumerate(ATT_DILATIONS):
        q_ref, kp_ref, kc_ref, kn_ref, vp_ref, vc_ref, vn_ref = groups[g]
        r = pos_tile // dil
        for rho in range(dil):
            kcat[0:ATT_HALF] = kp_ref[0, rho]
            kcat[ATT_HALF:ATT_HALF + r] = kc_ref[0, rho]
            kcat[ATT_HALF + r:2 * ATT_HALF + r] = kn_ref[0, rho]
            vcat[0:ATT_HALF] = vp_ref[0, rho]
            vcat[ATT_HALF:ATT_HALF + r] = vc_ref[0, rho]
            vcat[ATT_HALF + r:2 * ATT_HALF + r] = vn_ref[0, rho]
            for s in range(r // sub):
                q = q_ref[0, rho, s * sub:(s + 1) * sub, :]
                k = kcat[s * sub:(s + 2) * sub, :]
                v = vcat[s * sub:(s + 2) * sub, :]
                sc = lax.dot_general(q, k, (((1,), (1,)), ((), ())), preferred_element_type=F32)
                kidx = col + s * sub
                ok = band & ((kidx >= ATT_HALF) | (i > 0)) & ((kidx < r + ATT_HALF) | (i < last))
                sc = jnp.where(ok, sc, MASK_VALUE)
                m = jnp.max(sc, axis=-1, keepdims=True)
                p = jnp.exp(sc - m)
                l = jnp.sum(p, axis=-1, keepdims=True)
                acc = jnp.dot(p.astype(BF16), v, preferred_element_type=F32)
                start = rho + dil * sub * s
                if dil == 1:
                    dst = pl.ds(start, sub)
                else:
                    dst = pl.ds(start, sub, stride=dil)
                acc_n[g, dst, :] = acc
                m_n[g, dst, :] = jnp.broadcast_to(m, (sub, HEAD_DIM))
                l_n[g, dst, :] = jnp.broadcast_to(l, (sub, HEAD_DIM))
    m_all = jnp.maximum(jnp.maximum(m_n[0], m_n[1]), m_n[2])
    num = jnp.zeros_like(m_all)
    den = jnp.zeros_like(m_all)
    for g in range(3):
        w = jnp.exp(m_n[g] - m_all)
        num = num + w * acc_n[g]
        den = den + w * l_n[g]
    o_ref[...] = (num / den).astype(BF16)


def attention_branch(att_groups, *, batch, seq):
    p = min(ATT_POS_TILE, seq)
    assert seq % p == 0 and p % (ML_CHUNK * ATT_DILATIONS[-1]) == 0
    nblk = seq // p
    hb = ATT_HALF
    in_specs = []
    args = []
    for g, dil in enumerate(ATT_DILATIONS):
        r = p // dil
        s_len = seq // dil
        n_half_blocks = s_len // hb

        def cur(col0, r=r, dil=dil):
            return pl.BlockSpec((1, dil, r, HEAD_DIM), lambda b, h, i: (b, 0, i, col0 + h))

        def prev(col0, r=r, dil=dil):
            return pl.BlockSpec((1, dil, hb, HEAD_DIM),
                                lambda b, h, i: (b, 0, jnp.maximum(i * (r // hb) - 1, 0), col0 + h))

        def nxt(col0, r=r, dil=dil, nhb=n_half_blocks):
            return pl.BlockSpec((1, dil, hb, HEAD_DIM),
                                lambda b, h, i: (b, 0, jnp.minimum((i + 1) * (r // hb), nhb - 1), col0 + h))

        kc0, vc0 = ATT_HEADS_PER_GROUP, 2 * ATT_HEADS_PER_GROUP
        in_specs += [cur(0), prev(kc0), cur(kc0), nxt(kc0), prev(vc0), cur(vc0), nxt(vc0)]
        args += [att_groups[g]] * 7
    return pl.pallas_call(
        functools.partial(_attention_kernel, pos_tile=p),
        out_shape=jax.ShapeDtypeStruct((batch * seq, ATT_GROUP_W), BF16),
        grid=(batch, ATT_HEADS_PER_GROUP, nblk),
        in_specs=in_specs,
        out_specs=pl.BlockSpec((p, HEAD_DIM), lambda b, h, i: (b * nblk + i, h)),
        scratch_shapes=[pltpu.VMEM((p + 2 * hb, HEAD_DIM), BF16),
                        pltpu.VMEM((p + 2 * hb, HEAD_DIM), BF16),
                        pltpu.VMEM((3, p, HEAD_DIM), F32),
                        pltpu.VMEM((3, p, HEAD_DIM), F32),
                        pltpu.VMEM((3, p, HEAD_DIM), F32)],
        compiler_params=_cparams(("parallel", "parallel", "arbitrary")),
    )(*args)


def _conv_kernel(prev_ref, cur_ref, next_ref, w_ref, b_ref, o_ref, xcat, *, tt, tpb):
    i = pl.program_id(0)
    first = (i % tpb) == 0
    last = (i % tpb) == tpb - 1
    xcat[0:8] = jnp.where(first, 0.0, prev_ref[...])
    xcat[8:8 + tt] = cur_ref[...]
    xcat[8 + tt:16 + tt] = jnp.where(last, 0.0, next_ref[...])
    acc = jnp.broadcast_to(b_ref[...], (tt, b_ref.shape[1]))
    for j in range(ML_CONV_TAPS):
        acc = acc + w_ref[j:j + 1, :] * xcat[pl.ds(8 - ML_CONV_TAPS // 2 + j, tt), :]
    y = _silu(acc)
    lane = lax.broadcasted_iota(I32, y.shape, 1)
    o_ref[...] = jnp.where(lane < ML_QK_W, y * QK_SCALE, y).astype(BF16)


def mlstm_conv(qk_raw, conv_w, conv_b, *, seq):
    n, c = qk_raw.shape
    tt = min(512, seq)
    tpb = seq // tt
    w_pad = jnp.zeros((8, c), F32).at[:ML_CONV_TAPS].set(conv_w)
    return pl.pallas_call(
        functools.partial(_conv_kernel, tt=tt, tpb=tpb),
        out_shape=jax.ShapeDtypeStruct((n, c), BF16),
        grid=(n // tt,),
        in_specs=[pl.BlockSpec((8, c), lambda i: (jnp.maximum(i * (tt // 8) - 1, 0), 0)),
                  pl.BlockSpec((tt, c), lambda i: (i, 0)),
                  pl.BlockSpec((8, c), lambda i: (jnp.minimum((i + 1) * (tt // 8), n // 8 - 1), 0)),
                  pl.BlockSpec((8, c), lambda i: (0, 0)),
                  pl.BlockSpec((1, c), lambda i: (0, 0))],
        out_specs=pl.BlockSpec((tt, c), lambda i: (i, 0)),
        scratch_shapes=[pltpu.VMEM((tt + 16, c), F32)],
        compiler_params=_cparams(("parallel",)),
    )(qk_raw, qk_raw, qk_raw, w_pad, conv_b.reshape(1, c))


def _log_sigmoid(x):
    return jnp.minimum(x, 0.0) - jnp.log(1.0 + jnp.exp(-jnp.abs(x)))


def _mlstm_direction(q_ref, k_ref, v_ref, g_ref, bias_ref, h_ref, c_scr, n_scr, m_scr, *, reverse):
    L = ML_CHUNK
    row = lax.broadcasted_iota(I32, (L, L), 0)
    col = lax.broadcasted_iota(I32, (L, L), 1)
    causal = (col >= row) if reverse else (col <= row)
    gates = g_ref[...] + bias_ref[...]
    logf = _log_sigmoid(gates)
    b_cols = jnp.dot(causal.astype(F32), logf, preferred_element_type=F32,
                     precision=lax.Precision.HIGHEST)
    gates_t = gates.T
    b_rows = b_cols.T
    edge = 0 if reverse else L - 1
    off = 2 * ML_HEADS if reverse else 0
    for h in range(ML_HEADS):
        st = (ML_HEADS if reverse else 0) + h
        ci = off + h
        cf = off + ML_HEADS + h
        q = q_ref[:, h * HEAD_DIM:(h + 1) * HEAD_DIM]
        k = k_ref[:, h * HEAD_DIM:(h + 1) * HEAD_DIM]
        v = v_ref[:, h * ML_V_DIM:(h + 1) * ML_V_DIM]
        ib_r = gates_t[ci:ci + 1, :]
        ib_c = gates[:, ci:ci + 1]
        b_r = b_rows[cf:cf + 1, :]
        b_c = b_cols[:, cf:cf + 1]
        b_last = b_cols[edge:edge + 1, cf:cf + 1]
        m_prev = m_scr[st][:, 0:1]
        c_prev = c_scr[st]
        n_prev = n_scr[st]
        dlog = jnp.where(causal, b_c - b_r + ib_r, MASK_VALUE)
        inter = b_c + m_prev
        m_i = jnp.maximum(inter, jnp.max(dlog, axis=-1, keepdims=True))
        w = jnp.exp(dlog - m_i)
        s = lax.dot_general(q, k, (((1,), (1,)), ((), ())), preferred_element_type=F32) * w
        sc = jnp.exp(inter - m_i)
        num = jnp.dot(s.astype(BF16), v, preferred_element_type=F32)
        num = num + sc * jnp.dot(q, c_prev.astype(BF16), preferred_element_type=F32)
        qn = jnp.sum(q.astype(F32) * n_prev, axis=-1, keepdims=True)
        den = jnp.sum(s, axis=-1, keepdims=True) + sc * qn
        h_ref[:, h * ML_V_DIM:(h + 1) * ML_V_DIM] = num / jnp.maximum(jnp.abs(den), jnp.exp(-m_i))
        g_r = b_last - b_r + ib_r
        g_c = b_last - b_c + ib_c
        m_new = jnp.maximum(b_last + m_prev, jnp.max(g_r, axis=-1, keepdims=True))
        decay = jnp.exp(b_last + m_prev - m_new)
        kw = k.astype(F32) * jnp.exp(g_c - m_new)
        c_scr[st] = decay * c_prev + lax.dot_general(kw.astype(BF16), v, (((0,), (0,)), ((), ())),
                                                     preferred_element_type=F32)
        n_scr[st] = decay * n_prev + jnp.sum(kw, axis=0, keepdims=True)
        m_scr[st] = jnp.broadcast_to(m_new, (1, LANES))


def _mlstm_kernel(qf_ref, kf_ref, vf_ref, gf_ref, qb_ref, kb_ref, vb_ref, gb_ref, bias_ref,
                  hf_ref, hb_ref, c_scr, n_scr, m_scr):
    @pl.when(pl.program_id(1) == 0)
    def _():
        c_scr[...] = jnp.zeros_like(c_scr)
        n_scr[...] = jnp.zeros_like(n_scr)
        m_scr[...] = jnp.zeros_like(m_scr)

    _mlstm_direction(qf_ref, kf_ref, vf_ref, gf_ref, bias_ref, hf_ref, c_scr, n_scr, m_scr, reverse=False)
    _mlstm_direction(qb_ref, kb_ref, vb_ref, gb_ref, bias_ref, hb_ref, c_scr, n_scr, m_scr, reverse=True)


def mlstm_scan(qk, v, gates, bias_pad, *, batch, seq):
    n = batch * seq
    nc = seq // ML_CHUNK
    fwd = lambda col: (lambda b, c: (b * nc + c, col))
    bwd = lambda col: (lambda b, c: (b * nc + nc - 1 - c, col))

    def specs(at):
        return [pl.BlockSpec((ML_CHUNK, ML_QK_W), at(0)),
                pl.BlockSpec((ML_CHUNK, ML_QK_W), at(1)),
                pl.BlockSpec((ML_CHUNK, ML_V_W), at(0)),
                pl.BlockSpec((ML_CHUNK, GATE_PAD), at(0))]

    out = jax.ShapeDtypeStruct((n, ML_V_W), F32)
    return pl.pallas_call(
        _mlstm_kernel,
        out_shape=[out, out],
        grid=(batch, nc),
        in_specs=specs(fwd) + specs(bwd) + [pl.BlockSpec((1, GATE_PAD), lambda b, c: (0, 0))],
        out_specs=[pl.BlockSpec((ML_CHUNK, ML_V_W), fwd(0)), pl.BlockSpec((ML_CHUNK, ML_V_W), bwd(0))],
        scratch_shapes=[pltpu.VMEM((2 * ML_HEADS, HEAD_DIM, ML_V_DIM), F32),
                        pltpu.VMEM((2 * ML_HEADS, 1, HEAD_DIM), F32),
                        pltpu.VMEM((2 * ML_HEADS, 1, LANES), F32)],
        compiler_params=_cparams(("parallel", "arbitrary")),
    )(qk, qk, v, gates, qk, qk, v, gates, bias_pad)


def _postmix_kernel(att_ref, hf_ref, hb_ref, mlo_ref, mg_ref, x_ref, g1_ref, sc2_ref, sh2_ref,
                    n2w_ref, mlnw_ref, wml_ref, watt_ref, wo_ref, wrh_ref, wrl_ref,
                    x1_ref, h2_ref, lg_ref, *, d):
    hm = hf_ref[...] + hb_ref[...]
    parts = []
    for h in range(ML_HEADS):
        hh = hm[:, h * ML_V_DIM:(h + 1) * ML_V_DIM]
        parts.append(hh * lax.rsqrt(jnp.mean(hh * hh, axis=-1, keepdims=True) + NORM_EPS))
    hn = jnp.concatenate(parts, axis=1) * mlnw_ref[...]
    ml_act = (_sigmoid(mlo_ref[...].astype(F32)) * hn).astype(BF16)
    y_ml = jnp.dot(ml_act, wml_ref[...], preferred_element_type=F32)
    y_att = jnp.dot(att_ref[...], watt_ref[...], preferred_element_type=F32)
    g_ml = _sigmoid(mg_ref[:, 0:d].astype(F32))
    g_att = _sigmoid(mg_ref[:, d:2 * d].astype(F32))
    merged = (g_ml * y_ml + g_att * y_att).astype(BF16)
    out = jnp.dot(merged, wo_ref[...], preferred_element_type=F32)
    x1 = x_ref[...] + g1_ref[0] * out
    x1_ref[...] = x1
    ms = jnp.mean(x1 * x1, axis=-1, keepdims=True)
    h2 = x1 * lax.rsqrt(ms + NORM_EPS) * n2w_ref[...]
    h2 = h2 * (1.0 + sc2_ref[0]) + sh2_ref[0]
    h2_ref[...] = _pack_pair(h2[:, :d // 2], h2[:, d // 2:])
    h_hi = h2.astype(BF16)
    h_lo = (h2 - h_hi.astype(F32)).astype(BF16)
    lg_ref[...] = (jnp.dot(h_hi, wrh_ref[...], preferred_element_type=F32)
                   + jnp.dot(h_lo, wrh_ref[...], preferred_element_type=F32)
                   + jnp.dot(h_hi, wrl_ref[...], preferred_element_type=F32))


def post_mixer(att_o, h_f, h_b, ml_o, merge, x2d, gate1, scale2, shift2, norm2_w, ml_norm_w,
               w_ml_out, w_att_out, w_o, w_router_hi, w_router_lo, *, seq):
    n, d = x2d.shape
    tm = 256
    tpb = seq // tm
    const = lambda shape: pl.BlockSpec(shape, lambda i: tuple(0 for _ in shape), pipeline_mode=pl.Buffered(1))
    rows = lambda w: pl.BlockSpec((tm, w), lambda i: (i, 0))
    per_b = pl.BlockSpec((1, 1, d), lambda i: (i // tpb, 0, 0))
    return pl.pallas_call(
        functools.partial(_postmix_kernel, d=d),
        out_shape=[jax.ShapeDtypeStruct((n, d), F32),
                   jax.ShapeDtypeStruct((n, d // 2), I32),
                   jax.ShapeDtypeStruct((n, LANES), F32)],
        grid=(n // tm,),
        in_specs=[rows(ATT_GROUP_W), rows(ML_V_W), rows(ML_V_W), rows(ML_V_W), rows(2 * d), rows(d),
                  per_b, per_b, per_b, const((1, d)), const((1, ML_V_W)),
                  const((ML_V_W, d)), const((ATT_GROUP_W, d)), const((d, d)), const((d, LANES)), const((d, LANES))],
        out_specs=[rows(d), rows(d // 2), rows(LANES)],
        compiler_params=_cparams(("parallel",)),
    )(att_o, h_f, h_b, ml_o, merge, x2d, gate1, scale2, shift2, norm2_w, ml_norm_w,
      w_ml_out, w_att_out, w_o, w_router_hi, w_router_lo)


def _split3(x):
    p0 = x.astype(BF16)
    r1 = x - p0.astype(F32)
    p1 = r1.astype(BF16)
    p2 = (r1 - p1.astype(F32)).astype(BF16)
    return p0, p1, p2


def _route_select_kernel(lg_ref, gate_ref, dest_ref, lcs_ref, bp_ref, bs_ref, off_ref, cnt_ref, *, cap, n_tok):
    lg = lg_ref[...]
    e, nb, _ = lg.shape
    mx = jnp.max(lg, axis=0, keepdims=True)
    ex = jnp.exp(lg - mx)
    aff = ex / jnp.sum(ex, axis=0, keepdims=True)
    bits = lax.bitcast_convert_type(aff, I32)

    def count(pred):
        c = jnp.sum(jnp.where(pred, 1.0, 0.0), axis=1, keepdims=True)
        return jnp.sum(c, axis=2, keepdims=True)

    capf = jnp.float32(cap)

    def thr_step(_, c):
        lo, hi = c
        mid = lo + lax.shift_right_logical(hi - lo + 1, 1)
        ok = count(bits >= mid) >= capf
        return jnp.where(ok, mid, lo), jnp.where(ok, hi, mid - 1)

    lo0 = jnp.zeros((e, 1, 1), I32)
    hi0 = jnp.full((e, 1, 1), 0x7F800000, I32)
    thr, _ = lax.fori_loop(0, 31, thr_step, (lo0, hi0))
    gt = bits > thr
    eq = bits == thr
    need = capf - count(gt)
    nidx = (lax.broadcasted_iota(I32, (1, nb, LANES), 1) * LANES + lax.broadcasted_iota(I32, (1, nb, LANES), 2))

    def tie_step(_, c):
        lo, hi = c
        mid = lax.shift_right_logical(lo + hi, 1)
        ok = count(eq & (nidx < mid)) >= need
        return jnp.where(ok, lo, mid), jnp.where(ok, mid, hi)

    n_iter = max(1, (n_tok - 1).bit_length() + 1)
    _, jcut = lax.fori_loop(0, n_iter, tie_step, (jnp.zeros((e, 1, 1), I32), jnp.full((e, 1, 1), n_tok, I32)))
    mask = jnp.where(gt | (eq & (nidx < jcut)), 1.0, 0.0)
    gate_ref[...] = aff * mask

    jr = lax.broadcasted_iota(I32, (LANES, LANES), 0)
    jc = lax.broadcasted_iota(I32, (LANES, LANES), 1)
    tri_incl = jnp.where(jr <= jc, 1.0, 0.0).astype(BF16)
    br = lax.broadcasted_iota(I32, (nb, nb), 0)
    bc = lax.broadcasted_iota(I32, (nb, nb), 1)
    tri_prev = jnp.where(bc < br, 1.0, 0.0).astype(BF16)

    lcs = jnp.dot(mask.reshape(e * nb, LANES).astype(BF16), tri_incl, preferred_element_type=F32)
    lcs = lcs.reshape(e, nb, LANES)
    lcs_ref[...] = lcs.astype(BF16)
    cnt = jnp.zeros((nb, LANES), F32)
    ranks = []
    for ei in range(e):
        bs = jnp.broadcast_to(lcs[ei][:, LANES - 1:LANES], (nb, LANES))
        bs_ref[ei] = bs
        bp_ref[ei] = jnp.dot(tri_prev, bs.astype(BF16), preferred_element_type=F32)
        ranks.append(cnt)
        cnt = cnt + mask[ei]
    lcs_k = jnp.dot(cnt.astype(BF16), tri_incl, preferred_element_type=F32)
    bs_k = jnp.broadcast_to(lcs_k[:, LANES - 1:LANES], (nb, LANES))
    bs_hi = jnp.floor(bs_k * (1.0 / 64.0))
    bs_lo = bs_k - 64.0 * bs_hi
    bp_k = (64.0 * jnp.dot(tri_prev, bs_hi.astype(BF16), preferred_element_type=F32)
            + jnp.dot(tri_prev, bs_lo.astype(BF16), preferred_element_type=F32))
    off = bp_k + lcs_k - cnt
    off_ref[...] = off.astype(I32)
    cnt_ref[...] = cnt.astype(I32)
    for ei in range(e):
        dest_ref[ei] = jnp.where(mask[ei] > 0.0, off + ranks[ei], -1.0)


def _route_compact_kernel(lcs_ref, bp_ref, bs_ref, dest_ref, idx_out, dst_out, *, cap):
    lcs = lcs_ref[0]
    bp = bp_ref[0]
    bs = bs_ref[0]
    nb = lcs.shape[0]
    blk_id = lax.broadcasted_iota(I32, (nb, LANES), 0).astype(F32)
    lane = lax.broadcasted_iota(I32, (1, LANES), 1).astype(F32)
    oh_parts, base_parts, blk_parts = [], [], []
    for k in range(cap // LANES):
        r = lane + float(k * LANES)
        hit = (bp <= r) & (r < bp + bs)
        oh_parts.append(jnp.where(hit, 1.0, 0.0).astype(BF16))
        base_parts.append(jnp.sum(jnp.where(hit, bp, 0.0), axis=0, keepdims=True))
        blk_parts.append(jnp.sum(jnp.where(hit, blk_id, 0.0), axis=0, keepdims=True))
    onehot = jnp.concatenate(oh_parts, axis=1)
    base = jnp.concatenate(base_parts, axis=1)
    blk = jnp.concatenate(blk_parts, axis=1)
    rows = lax.broadcasted_iota(I32, (1, cap), 1).astype(F32)
    contract0 = (((0,), (0,)), ((), ()))
    counts = lax.dot_general(lcs, onehot, contract0, preferred_element_type=F32)
    jloc = jnp.sum(jnp.where(counts <= rows - base, 1.0, 0.0), axis=0, keepdims=True)
    dvals = jnp.zeros((LANES, cap), F32)
    for piece in _split3(dest_ref[0]):
        dvals = dvals + lax.dot_general(piece, onehot, contract0, preferred_element_type=F32)
    jcol = lax.broadcasted_iota(I32, (LANES, cap), 0).astype(F32)
    dst = jnp.sum(jnp.where(jcol == jloc, dvals, 0.0), axis=0, keepdims=True)
    idx_out[0] = (blk * float(LANES) + jloc).astype(I32)
    dst_out[0] = dst.astype(I32)


def route_tokens(logits, n_experts):
    n = logits.shape[0]
    e = n_experts
    nb = n // LANES
    cap = CAPACITY_FACTOR * n // e
    assert n % LANES == 0 and cap % LANES == 0
    lg = logits[:, :e].T.reshape(e, nb, LANES)
    big = lambda dt: jax.ShapeDtypeStruct((e, nb, LANES), dt)
    small = jax.ShapeDtypeStruct((nb, LANES), I32)
    gate, dest, lcs, bp, bs, off, cnt = pl.pallas_call(
        functools.partial(_route_select_kernel, cap=cap, n_tok=n),
        out_shape=[big(F32), big(F32), big(BF16), big(F32), big(F32), small, small],
        compiler_params=pltpu.CompilerParams(vmem_limit_bytes=VMEM_LIMIT),
    )(lg)
    per_e = lambda: pl.BlockSpec((1, nb, LANES), lambda i: (i, 0, 0))
    idx, dst = pl.pallas_call(
        functools.partial(_route_compact_kernel, cap=cap),
        out_shape=[jax.ShapeDtypeStruct((e, 1, cap), I32), jax.ShapeDtypeStruct((e, 1, cap), I32)],
        grid=(e,),
        in_specs=[per_e(), per_e(), per_e(), per_e()],
        out_specs=[pl.BlockSpec((1, 1, cap), lambda i: (i, 0, 0)), pl.BlockSpec((1, 1, cap), lambda i: (i, 0, 0))],
        compiler_params=_cparams(("parallel",)),
    )(lcs, bp, bs, dest)
    gate_tok = gate.reshape(e, n).T
    dest_tok = dest.reshape(e, n).T.astype(I32)
    return idx.reshape(-1), dst.reshape(-1), gate_tok, dest_tok, off.reshape(-1), cnt.reshape(-1)


def _expert_kernel(idx_ref, idxn_ref, dst_ref, wg_ref, wu_ref, wd_ref, h2_hbm, z_hbm,
                   xbuf, zbuf, xs, acc, gsem, ssem, *, d, ff, tr):
    s = pl.program_id(0)
    nsteps = pl.num_programs(0)
    half = d // 2

    def start_gather(ids, rows=range(tr)):
        for r in rows:
            pltpu.make_async_copy(h2_hbm.at[pl.ds(ids[0, 0, r], 1), :], xbuf.at[pl.ds(r, 1), :], gsem.at[0]).start()

    def wait_gather():
        pltpu.make_async_copy(h2_hbm.at[pl.ds(0, tr), :], xbuf, gsem.at[0]).wait()

    def wait_scatter():
        pltpu.make_async_copy(zbuf, z_hbm.at[pl.ds(0, tr), :], ssem.at[0]).wait()

    @pl.when(s == 0)
    def _():
        start_gather(idx_ref)

    wait_gather()
    lo, hi = _unpack_pair(xbuf[...])
    xs[:, :half] = lo.astype(BF16)
    xs[:, half:] = hi.astype(BF16)
    x = xs[...]
    tf = min(ff, 512)
    n_groups = 3 * (ff // tf)
    per_group = -(-tr // n_groups)
    groups = [range(i * per_group, min((i + 1) * per_group, tr)) for i in range(n_groups)]
    for f in range(ff // tf):
        g = jnp.dot(x, wg_ref[0, :, f * tf:(f + 1) * tf], preferred_element_type=F32)
        start_gather(idxn_ref, groups[3 * f])
        u = jnp.dot(x, wu_ref[0, :, f * tf:(f + 1) * tf], preferred_element_type=F32)
        start_gather(idxn_ref, groups[3 * f + 1])
        hid = (_silu(g) * u).astype(BF16)
        part = jnp.dot(hid, wd_ref[0, f * tf:(f + 1) * tf, :], preferred_element_type=F32)
        start_gather(idxn_ref, groups[3 * f + 2])
        if f == 0:
            acc[...] = part
        else:
            acc[...] += part

    @pl.when(s > 0)
    def _():
        wait_scatter()

    zbuf[...] = _pack_pair(acc[:, :half], acc[:, half:])
    for r in range(tr):
        pltpu.make_async_copy(zbuf.at[pl.ds(r, 1), :], z_hbm.at[pl.ds(dst_ref[0, 0, r], 1), :], ssem.at[0]).start()

    @pl.when(s == nsteps - 1)
    def _():
        wait_scatter()
        wait_gather()


def expert_ffn(h2_packed, idx, dest, w_gate, w_up, w_down):
    n, half = h2_packed.shape
    e, d, ff = w_gate.shape
    total = idx.shape[0]
    cap = total // e
    tr = min(EXPERT_ROWS, cap)
    nsteps = total // tr
    per_e = cap // tr
    idx3 = idx.reshape(nsteps, 1, tr)
    dst3 = dest.reshape(nsteps, 1, tr)
    smem = lambda f: pl.BlockSpec((1, 1, tr), f, memory_space=pltpu.SMEM)
    wspec = lambda shape: pl.BlockSpec(shape, lambda s: (s // per_e, 0, 0), pipeline_mode=pl.Buffered(1))
    return pl.pallas_call(
        functools.partial(_expert_kernel, d=d, ff=ff, tr=tr),
        out_shape=jax.ShapeDtypeStruct((total, half), I32),
        grid=(nsteps,),
        in_specs=[smem(lambda s: (s, 0, 0)),
                  smem(lambda s: (jnp.minimum(s + 1, nsteps - 1), 0, 0)),
                  smem(lambda s: (s, 0, 0)),
                  wspec((1, d, ff)), wspec((1, d, ff)), wspec((1, ff, d)),
                  pl.BlockSpec(memory_space=pl.ANY)],
        out_specs=pl.BlockSpec(memory_space=pl.ANY),
        scratch_shapes=[pltpu.VMEM((tr, half), I32),
                        pltpu.VMEM((tr, half), I32),
                        pltpu.VMEM((tr, d), BF16),
                        pltpu.VMEM((tr, d), F32),
                        pltpu.SemaphoreType.DMA((1,)),
                        pltpu.SemaphoreType.DMA((1,))],
        compiler_params=_cparams(("arbitrary",)),
    )(idx3, idx3, dst3, w_gate, w_up, w_down, h2_packed)


def _combine_kernel(tile_ref, chunk_ref, flag_ref, z_ref, gt_ref, dt_ref, x_ref, g2_ref, fw_ref, o_ref,
                    acc, *, d, tm, tr, n_experts, final):
    w = pl.program_id(0)
    flags = flag_ref[w]

    @pl.when((flags & 2) != 0)
    def _():
        acc[...] = jnp.zeros_like(acc)

    @pl.when((flags & 1) != 0)
    def _():
        lo, hi = _unpack_pair(z_ref[...])
        zs = jnp.concatenate([lo.astype(BF16), hi.astype(BF16)], axis=1)
        zrow = lax.broadcasted_iota(I32, (tm, tr), 1) + chunk_ref[w] * tr
        gate = gt_ref[...]
        dest = dt_ref[...]
        sel = jnp.zeros((tm, tr), F32)
        for ei in range(n_experts):
            sel = sel + jnp.where(zrow == dest[:, ei:ei + 1], gate[:, ei:ei + 1], 0.0)
        sel_hi = sel.astype(BF16)
        sel_lo = (sel - sel_hi.astype(F32)).astype(BF16)
        acc[...] += (jnp.dot(sel_hi, zs, preferred_element_type=F32)
                     + jnp.dot(sel_lo, zs, preferred_element_type=F32))

    @pl.when((flags & 4) != 0)
    def _():
        x2 = x_ref[...] + g2_ref[0] * acc[...]
        if final:
            ms = jnp.mean(x2 * x2, axis=-1, keepdims=True)
            x2 = x2 * lax.rsqrt(ms + NORM_EPS) * fw_ref[...]
        o_ref[...] = x2


def moe_combine(z_sorted, item_tile, item_chunk, item_flags, gate_tok, dest_tok, x1, gate2, final_w, *, seq, final):
    n, d = x1.shape
    total, half = z_sorted.shape
    e = gate_tok.shape[1]
    tm = min(MOE_ROWS, n)
    tr = min(MOE_ROWS, total)
    tpb = seq // tm
    n_items = item_tile.shape[0]
    return pl.pallas_call(
        functools.partial(_combine_kernel, d=d, tm=tm, tr=tr, n_experts=e, final=final),
        out_shape=jax.ShapeDtypeStruct((n, d), F32),
        grid_spec=pltpu.PrefetchScalarGridSpec(
            num_scalar_prefetch=3,
            grid=(n_items,),
            in_specs=[pl.BlockSpec((tr, half), lambda w, t, c, f: (c[w], 0)),
                      pl.BlockSpec((tm, e), lambda w, t, c, f: (t[w], 0)),
                      pl.BlockSpec((tm, e), lambda w, t, c, f: (t[w], 0)),
                      pl.BlockSpec((tm, d), lambda w, t, c, f: (t[w], 0)),
                      pl.BlockSpec((1, 1, d), lambda w, t, c, f: (t[w] // tpb, 0, 0)),
                      pl.BlockSpec((1, d), lambda w, t, c, f: (0, 0))],
            out_specs=pl.BlockSpec((tm, d), lambda w, t, c, f: (t[w], 0)),
            scratch_shapes=[pltpu.VMEM((tm, d), F32)]),
        compiler_params=_cparams(("arbitrary",)),
    )(item_tile, item_chunk, item_flags, z_sorted, gate_tok, dest_tok, x1, gate2, final_w)


def _combine_items(off, n, total, tm, tr):
    nt = n // tm
    nchunks = total // tr
    first = off.reshape(nt, tm)[:, 0]
    end = jnp.concatenate([first[1:], jnp.array([total], I32)])
    c_lo = jnp.minimum(first // tr, nchunks - 1)
    c_hi = jnp.where(end > first, (end - 1) // tr, c_lo)
    cnt = c_hi - c_lo + 1
    start = jnp.cumsum(cnt) - cnt
    n_items = nt + nchunks
    w = jnp.arange(n_items, dtype=I32)
    t = jnp.clip(jnp.searchsorted(start, w, side="right").astype(I32) - 1, 0, nt - 1)
    k = w - start[t]
    valid = k < cnt[t]
    k = jnp.minimum(k, cnt[t] - 1)
    flags = jnp.where(valid, 1 + 2 * (k == 0) + 4 * (k == cnt[t] - 1), 0)
    return t.astype(I32), (c_lo[t] + k).astype(I32), flags.astype(I32)


def _rope_tables(seq):
    freqs = jnp.float32(ROT_THETA) ** (-jnp.arange(ROT_HALF, dtype=F32) / ROT_HALF)
    ang = jnp.arange(seq, dtype=F32)[:, None] * freqs[None, :]
    cos = jnp.cos(ang)
    sin = jnp.sin(ang)
    ones = jnp.ones((seq, HEAD_DIM - 2 * ROT_HALF), F32)
    cos_t = jnp.concatenate([cos, cos, ones], axis=1)
    sin_t = jnp.concatenate([-sin, sin, 0.0 * ones], axis=1)
    return cos_t, sin_t


def _prep_layer(p, l):
    w_in = p["w_in"][l]
    aw = 3 * ATT_GROUP_W
    q, k, v = w_in[:, 0:aw], w_in[:, aw:2 * aw], w_in[:, 2 * aw:3 * aw]
    cols = []
    for g in range(3):
        sl = slice(g * ATT_GROUP_W, (g + 1) * ATT_GROUP_W)
        cols += [q[:, sl], k[:, sl], v[:, sl]]
    o = 3 * aw
    cols.append(w_in[:, o:o + 2 * ML_QK_W + 2 * ML_V_W])
    o += 2 * ML_QK_W + 2 * ML_V_W
    w_gates = w_in[:, o:o + 4 * ML_HEADS]
    o += 4 * ML_HEADS
    cols.append(w_in[:, o:])
    d = w_in.shape[0]
    w_router = jnp.zeros((d, LANES), F32).at[:, :N_EXPERTS].set(p["w_router"][l])
    return dict(
        w_main=jnp.concatenate(cols, axis=1).astype(BF16),
        w_gates=jnp.zeros((d, GATE_PAD), BF16).at[:, :4 * ML_HEADS].set(w_gates.astype(BF16)),
        bias_pad=jnp.zeros((1, GATE_PAD), F32).at[0, :4 * ML_HEADS].set(p["b_mgate"][l]),
        w_ml_out=p["w_ml_out"][l].astype(BF16),
        w_att_out=p["w_att_out"][l].astype(BF16),
        w_o=p["w_o"][l].astype(BF16),
        w_router_hi=w_router.astype(BF16),
        w_router_lo=(w_router - w_router.astype(BF16).astype(F32)).astype(BF16),
        w_exp_gate=p["w_exp_gate"][l].astype(BF16),
        w_exp_up=p["w_exp_up"][l].astype(BF16),
        w_exp_down=p["w_exp_down"][l].astype(BF16),
    )


def _moe_sublayer(x1, h2p, logits, gate2, lw, final_w, *, seq, final):
    n = x1.shape[0]
    idx, dest, gate_tok, dest_tok, off, _ = route_tokens(logits, N_EXPERTS)
    z = expert_ffn(h2p, idx, dest, lw["w_exp_gate"], lw["w_exp_up"], lw["w_exp_down"])
    tm = min(MOE_ROWS, n)
    tr = min(MOE_ROWS, z.shape[0])
    items = _combine_items(off, n, z.shape[0], tm, tr)
    return moe_combine(z, *items, gate_tok, dest_tok, x1, gate2, final_w, seq=seq, final=final)


def _run_group(x, mods, layers, p, cos_t, sin_t):
    batch, seq, d = x.shape
    n = batch * seq
    x2d = x.reshape(n, d)
    depth = len(layers)
    for l, lw in enumerate(layers):
        shift1, scale1, gate1, shift2, scale2, gate2 = [m.reshape(batch, 1, d) for m in jnp.split(mods[l], 6, axis=-1)]
        a0, a1, a2, ml_qk, ml_v, ml_o, merge, gates = input_projection(
            x2d, scale1, shift1, p["norm1_w"][l].reshape(1, d), cos_t, sin_t, lw["w_main"], lw["w_gates"],
            batch=batch, seq=seq)
        att_o = attention_branch((a0, a1, a2), batch=batch, seq=seq)
        qk = mlstm_conv(ml_qk, p["conv_w"][l], p["conv_b"][l], seq=seq)
        h_f, h_b = mlstm_scan(qk, ml_v, gates, lw["bias_pad"], batch=batch, seq=seq)
        x1, h2p, logits = post_mixer(att_o, h_f, h_b, ml_o, merge, x2d, gate1, scale2, shift2,
                                     p["norm2_w"][l].reshape(1, d), p["ml_norm_w"][l].reshape(1, ML_V_W),
                                     lw["w_ml_out"], lw["w_att_out"], lw["w_o"], lw["w_router_hi"], lw["w_router_lo"], seq=seq)
        x2d = _moe_sublayer(x1, h2p, logits, gate2, lw, p["final_norm_w"].reshape(1, d),
                            seq=seq, final=(l == depth - 1))
    return x2d.reshape(batch, seq, d)


def kernel(x_prompt, x_sample, c_prompt, c_sample, norm1_w, norm2_w, w_ada, b_ada, w_in, b_mgate, conv_w, conv_b, ml_norm_w, w_ml_out, w_att_out, w_o, w_router, w_exp_gate, w_exp_up, w_exp_down, final_norm_w):
    p = dict(norm1_w=norm1_w, norm2_w=norm2_w, w_in=w_in, b_mgate=b_mgate, conv_w=conv_w, conv_b=conv_b,
             ml_norm_w=ml_norm_w, w_ml_out=w_ml_out, w_att_out=w_att_out, w_o=w_o, w_router=w_router,
             w_exp_gate=w_exp_gate, w_exp_up=w_exp_up, w_exp_down=w_exp_down, final_norm_w=final_norm_w)
    depth = w_in.shape[0]
    bp, bs = c_prompt.shape[0], c_sample.shape[0]
    m_pad = -(-(bp + bs) // 8) * 8
    c_pad = jnp.zeros((m_pad, c_prompt.shape[1]), F32).at[:bp].set(c_prompt).at[bp:bp + bs].set(c_sample)
    mods = ada_modulation(c_pad, w_ada, b_ada)
    layers = [_prep_layer(p, l) for l in range(depth)]
    outs = []
    for x, rows in ((x_prompt, slice(0, bp)), (x_sample, slice(bp, bp + bs))):
        cos_t, sin_t = _rope_tables(x.shape[1])
        outs.append(_run_group(x, mods[:, rows], layers, p, cos_t, sin_t))
    return tuple(outs)
```

```python
import functools
import math

import jax
import jax.numpy as jnp
from jax import lax
from jax.experimental import pallas as pl
from jax.experimental.pallas import tpu as pltpu

F32 = jnp.float32
BF16 = jnp.bfloat16
I32 = jnp.int32

NORM_EPS = 1e-6
MASK_VALUE = -1e30
HEAD_DIM = 128
ATT_DILATIONS = (1, 4, 16)
ATT_WINDOWS = (128, 512, 2048)
ATT_HALF = 64
ATT_HEADS_PER_GROUP = 4
ATT_GROUP_W = ATT_HEADS_PER_GROUP * HEAD_DIM
ROT_HALF = 16
ROT_THETA = 500000.0
ML_HEADS = 4
ML_V_DIM = 256
ML_QK_W = ML_HEADS * HEAD_DIM
ML_V_W = ML_HEADS * ML_V_DIM
ML_CHUNK = 128
ML_CONV_TAPS = 5
N_EXPERTS = 16
CAPACITY_FACTOR = 2
GATE_PAD = 128
LANES = 128
VMEM_LIMIT = 56 * 1024 * 1024
QK_SCALE = HEAD_DIM ** -0.5

PROJ_TN = 512
PROJ_ROWS = 256
ATT_POS_TILE = 2048
MOE_ROWS = 256
EXPERT_ROWS = 512


def _cparams(sem):
    return pltpu.CompilerParams(dimension_semantics=sem, vmem_limit_bytes=VMEM_LIMIT)


def _sigmoid(x):
    return 1.0 / (1.0 + jnp.exp(-x))


def _silu(x):
    return x * _sigmoid(x)


def _bf16_bits(x):
    b = lax.bitcast_convert_type(x, I32)
    return b + 0x7FFF + (lax.shift_right_logical(b, 16) & 1)


def _pack_pair(lo, hi):
    return (_bf16_bits(hi) & (-65536)) | lax.shift_right_logical(_bf16_bits(lo), 16)


def _unpack_pair(w):
    lo = lax.bitcast_convert_type(lax.shift_left(w, 16), F32)
    hi = lax.bitcast_convert_type(w & (-65536), F32)
    return lo, hi


def _ada_kernel(c_ref, w_ref, b_ref, o_ref):
    c = c_ref[...]
    a = _silu(c).astype(BF16)
    o_ref[0] = jnp.dot(a, w_ref[0].astype(BF16), preferred_element_type=F32) + b_ref[0]


def ada_modulation(c_pad, w_ada, b_ada):
    m, d = c_pad.shape
    depth, _, n = w_ada.shape
    tn = min(n, 1536)
    return pl.pallas_call(
        _ada_kernel,
        out_shape=jax.ShapeDtypeStruct((depth, m, n), F32),
        grid=(depth, n // tn),
        in_specs=[pl.BlockSpec((m, d), lambda l, j: (0, 0)),
                  pl.BlockSpec((1, d, tn), lambda l, j: (l, 0, j)),
                  pl.BlockSpec((1, 1, tn), lambda l, j: (l, 0, j))],
        out_specs=pl.BlockSpec((1, m, tn), lambda l, j: (l, 0, j)),
        compiler_params=_cparams(("parallel", "parallel")),
    )(c_pad, w_ada, b_ada.reshape(depth, 1, n))


def _rope(a, cos, sin_signed):
    up = pltpu.roll(a, HEAD_DIM - ROT_HALF, axis=1)
    dn = pltpu.roll(a, ROT_HALF, axis=1)
    lane = lax.broadcasted_iota(I32, a.shape, 1)
    partner = jnp.where(lane < ROT_HALF, up, dn)
    return a * cos + partner * sin_signed


def _inproj_kernel(x_ref, sc_ref, sh_ref, nw_ref, cos_ref, sin_ref, w_ref, wg_ref,
                   a0_ref, a1_ref, a2_ref, qk_ref, v_ref, o_ref, mg_ref, gt_ref,
                   h_scr, acc_scr, *, tm):
    j = pl.program_id(1)

    @pl.when(j == 0)
    def _():
        x = x_ref[...]
        ms = jnp.mean(x * x, axis=-1, keepdims=True)
        h = x * lax.rsqrt(ms + NORM_EPS) * nw_ref[...]
        h = h * (1.0 + sc_ref[0]) + sh_ref[0]
        hb = h.astype(BF16)
        h_scr[...] = hb
        gt_ref[...] = jnp.dot(hb, wg_ref[...], preferred_element_type=F32)

    row_chunks = [(r0, min(PROJ_ROWS, tm - r0)) for r0 in range(0, tm, PROJ_ROWS)]

    def project(r0, nr):
        return jnp.dot(h_scr[r0:r0 + nr, :], w_ref[0], preferred_element_type=F32)

    att_refs = (a0_ref, a1_ref, a2_ref)
    for g, dil in enumerate(ATT_DILATIONS):
        for t in range(3):

            @pl.when(j == 3 * g + t)
            def _(g=g, dil=dil, t=t):
                out = att_refs[g]
                for r0, nr in row_chunks:
                    a = project(r0, nr)
                    if t < 2:
                        cos = cos_ref[r0:r0 + nr, :]
                        sin = sin_ref[r0:r0 + nr, :]
                        heads = []
                        for hh in range(ATT_HEADS_PER_GROUP):
                            ah = a[:, hh * HEAD_DIM:(hh + 1) * HEAD_DIM]
                            if t == 0:
                                ah = ah * QK_SCALE
                            heads.append(_rope(ah, cos, sin))
                        a = jnp.concatenate(heads, axis=1)
                    if dil == 1:
                        out[0, 0, r0:r0 + nr, :] = a.astype(BF16)
                    else:
                        for hh in range(ATT_HEADS_PER_GROUP):
                            acc_scr[hh, r0:r0 + nr, :] = a[:, hh * HEAD_DIM:(hh + 1) * HEAD_DIM]
                        for rho in range(dil):
                            for hh in range(ATT_HEADS_PER_GROUP):
                                out[0, rho, r0 // dil:(r0 + nr) // dil, hh * HEAD_DIM:(hh + 1) * HEAD_DIM] = (
                                    acc_scr[hh, pl.ds(r0 + rho, nr // dil, stride=dil), :].astype(BF16))

    @pl.when((j == 9) | (j == 10))
    def _():
        for r0, nr in row_chunks:
            qk_ref[r0:r0 + nr, :] = project(r0, nr)

    @pl.when((j == 11) | (j == 12))
    def _():
        for r0, nr in row_chunks:
            v_ref[r0:r0 + nr, :] = project(r0, nr).astype(BF16)

    @pl.when((j == 13) | (j == 14))
    def _():
        for r0, nr in row_chunks:
            o_ref[r0:r0 + nr, :] = project(r0, nr).astype(BF16)

    @pl.when(j >= 15)
    def _():
        for r0, nr in row_chunks:
            mg_ref[r0:r0 + nr, :] = project(r0, nr).astype(BF16)


def input_projection(x2d, scale, shift, norm_w, cos_t, sin_t, w_main, w_gates, *, batch, seq):
    n, d = x2d.shape
    tm = min(1024, seq)
    tpb = seq // tm
    nj = w_main.shape[0]
    d_merge = (nj - (9 + 6)) * PROJ_TN
    assert nj == 15 + d_merge // PROJ_TN and seq % tm == 0 and tm % (16 * 16) == 0

    def att_spec(g, dil):
        return pl.BlockSpec((1, dil, tm // dil, PROJ_TN),
                            lambda i, j: (i // tpb, 0, i % tpb, jnp.clip(j - 3 * g, 0, 2)))

    def col_spec(first, count):
        return pl.BlockSpec((tm, PROJ_TN), lambda i, j: (i, jnp.clip(j - first, 0, count - 1)))

    out_shape = [jax.ShapeDtypeStruct((batch, dil, seq // dil, 3 * ATT_GROUP_W), BF16) for dil in ATT_DILATIONS]
    out_shape += [jax.ShapeDtypeStruct((n, 2 * ML_QK_W), F32),
                  jax.ShapeDtypeStruct((n, ML_V_W), BF16),
                  jax.ShapeDtypeStruct((n, ML_V_W), BF16),
                  jax.ShapeDtypeStruct((n, d_merge), BF16),
                  jax.ShapeDtypeStruct((n, GATE_PAD), F32)]
    out_specs = [att_spec(g, dil) for g, dil in enumerate(ATT_DILATIONS)]
    out_specs += [col_spec(9, 2), col_spec(11, 2), col_spec(13, 2), col_spec(15, d_merge // PROJ_TN),
                  pl.BlockSpec((tm, GATE_PAD), lambda i, j: (i, 0))]
    return pl.pallas_call(
        functools.partial(_inproj_kernel, tm=tm),
        out_shape=out_shape,
        grid=(n // tm, nj),
        in_specs=[pl.BlockSpec((tm, d), lambda i, j: (i, 0), pipeline_mode=pl.Buffered(1)),
                  pl.BlockSpec((1, 1, d), lambda i, j: (i // tpb, 0, 0)),
                  pl.BlockSpec((1, 1, d), lambda i, j: (i // tpb, 0, 0)),
                  pl.BlockSpec((1, d), lambda i, j: (0, 0)),
                  pl.BlockSpec((tm, HEAD_DIM), lambda i, j: (i % tpb, 0)),
                  pl.BlockSpec((tm, HEAD_DIM), lambda i, j: (i % tpb, 0)),
                  pl.BlockSpec((1, d, PROJ_TN), lambda i, j: (j, 0, 0)),
                  pl.BlockSpec((d, GATE_PAD), lambda i, j: (0, 0))],
        out_specs=out_specs,
        scratch_shapes=[pltpu.VMEM((tm, d), BF16), pltpu.VMEM((ATT_HEADS_PER_GROUP, tm, HEAD_DIM), F32)],
        compiler_params=_cparams(("parallel", "arbitrary")),
    )(x2d, scale, shift, norm_w, cos_t, sin_t, w_main, w_gates)


def _attention_kernel(*refs, pos_tile):
    groups = [refs[7 * g:7 * g + 7] for g in range(3)]
    o_ref = refs[21]
    kcat, vcat, acc_n, m_n, l_n = refs[22:27]
    i = pl.program_id(2)
    last = pl.num_programs(2) - 1
    sub = ML_CHUNK
    row = lax.broadcasted_iota(I32, (sub, 2 * sub), 0)
    col = lax.broadcasted_iota(I32, (sub, 2 * sub), 1)
    band = (col >= row) & (col - row <= 2 * ATT_HALF)
    for g, dil in enumerate(ATT_DILATIONS):
        q_ref, kp_ref, kc_ref, kn_ref, vp_ref, vc_ref, vn_ref = groups[g]
        r = pos_tile // dil
        for rho in range(dil):
            kcat[0:ATT_HALF] = kp_ref[0, rho]
            kcat[ATT_HALF:ATT_HALF + r] = kc_ref[0, rho]
            kcat[ATT_HALF + r:2 * ATT_HALF + r] = kn_ref[0, rho]
            vcat[0:ATT_HALF] = vp_ref[0, rho]
            vcat[ATT_HALF:ATT_HALF + r] = vc_ref[0, rho]
            vcat[ATT_HALF + r:2 * ATT_HALF + r] = vn_ref[0, rho]
            for s in range(r // sub):
                q = q_ref[0, rho, s * sub:(s + 1) * sub, :]
                k = kcat[s * sub:(s + 2) * sub, :]
                v = vcat[s * sub:(s + 2) * sub, :]
                sc = lax.dot_general(q, k, (((1,), (1,)), ((), ())), preferred_element_type=F32)
                kidx = col + s * sub
                ok = band & ((kidx >= ATT_HALF) | (i > 0)) & ((kidx < r + ATT_HALF) | (i < last))
                sc = jnp.where(ok, sc, MASK_VALUE)
                m = jnp.max(sc, axis=-1, keepdims=True)
                p = jnp.exp(sc - m)
                l = jnp.sum(p, axis=-1, keepdims=True)
                acc = jnp.dot(p.astype(BF16), v, preferred_element_type=F32)
                start = rho + dil * sub * s
                if dil == 1:
                    dst = pl.ds(start, sub)
                else:
                    dst = pl.ds(start, sub, stride=dil)
                acc_n[g, dst, :] = acc
                m_n[g, dst, :] = jnp.broadcast_to(m, (sub, HEAD_DIM))
                l_n[g, dst, :] = jnp.broadcast_to(l, (sub, HEAD_DIM))
    m_all = jnp.maximum(jnp.maximum(m_n[0], m_n[1]), m_n[2])
    num = jnp.zeros_like(m_all)
    den = jnp.zeros_like(m_all)
    for g in range(3):
        w = jnp.exp(m_n[g] - m_all)
        num = num + w * acc_n[g]
        den = den + w * l_n[g]
    o_ref[...] = (num / den).astype(BF16)


def attention_branch(att_groups, *, batch, seq):
    p = min(ATT_POS_TILE, seq)
    assert seq % p == 0 and p % (ML_CHUNK * ATT_DILATIONS[-1]) == 0
    nblk = seq // p
    hb = ATT_HALF
    in_specs = []
    args = []
    for g, dil in enumerate(ATT_DILATIONS):
        r = p // dil
        s_len = seq // dil
        n_half_blocks = s_len // hb

        def cur(col0, r=r, dil=dil):
            return pl.BlockSpec((1, dil, r, HEAD_DIM), lambda b, h, i: (b, 0, i, col0 + h))

        def prev(col0, r=r, dil=dil):
            return pl.BlockSpec((1, dil, hb, HEAD_DIM),
                                lambda b, h, i: (b, 0, jnp.maximum(i * (r // hb) - 1, 0), col0 + h))

        def nxt(col0, r=r, dil=dil, nhb=n_half_blocks):
            return pl.BlockSpec((1, dil, hb, HEAD_DIM),
                                lambda b, h, i: (b, 0, jnp.minimum((i + 1) * (r // hb), nhb - 1), col0 + h))

        kc0, vc0 = ATT_HEADS_PER_GROUP, 2 * ATT_HEADS_PER_GROUP
        in_specs += [cur(0), prev(kc0), cur(kc0), nxt(kc0), prev(vc0), cur(vc0), nxt(vc0)]
        args += [att_groups[g]] * 7
    return pl.pallas_call(
        functools.partial(_attention_kernel, pos_tile=p),
        out_shape=jax.ShapeDtypeStruct((batch * seq, ATT_GROUP_W), BF16),
        grid=(batch, ATT_HEADS_PER_GROUP, nblk),
        in_specs=in_specs,
        out_specs=pl.BlockSpec((p, HEAD_DIM), lambda b, h, i: (b * nblk + i, h)),
        scratch_shapes=[pltpu.VMEM((p + 2 * hb, HEAD_DIM), BF16),
                        pltpu.VMEM((p + 2 * hb, HEAD_DIM), BF16),
                        pltpu.VMEM((3, p, HEAD_DIM), F32),
                        pltpu.VMEM((3, p, HEAD_DIM), F32),
                        pltpu.VMEM((3, p, HEAD_DIM), F32)],
        compiler_params=_cparams(("parallel", "parallel", "arbitrary")),
    )(*args)


def _conv_kernel(prev_ref, cur_ref, next_ref, w_ref, b_ref, o_ref, xcat, *, tt, tpb):
    i = pl.program_id(0)
    first = (i % tpb) == 0
    last = (i % tpb) == tpb - 1
    xcat[0:8] = jnp.where(first, 0.0, prev_ref[...])
    xcat[8:8 + tt] = cur_ref[...]
    xcat[8 + tt:16 + tt] = jnp.where(last, 0.0, next_ref[...])
    acc = jnp.broadcast_to(b_ref[...], (tt, b_ref.shape[1]))
    for j in range(ML_CONV_TAPS):
        acc = acc + w_ref[j:j + 1, :] * xcat[pl.ds(8 - ML_CONV_TAPS // 2 + j, tt), :]
    y = _silu(acc)
    lane = lax.broadcasted_iota(I32, y.shape, 1)
    o_ref[...] = jnp.where(lane < ML_QK_W, y * QK_SCALE, y).astype(BF16)


def mlstm_conv(qk_raw, conv_w, conv_b, *, seq):
    n, c = qk_raw.shape
    tt = min(512, seq)
    tpb = seq // tt
    w_pad = jnp.zeros((8, c), F32).at[:ML_CONV_TAPS].set(conv_w)
    return pl.pallas_call(
        functools.partial(_conv_kernel, tt=tt, tpb=tpb),
        out_shape=jax.ShapeDtypeStruct((n, c), BF16),
        grid=(n // tt,),
        in_specs=[pl.BlockSpec((8, c), lambda i: (jnp.maximum(i * (tt // 8) - 1, 0), 0)),
                  pl.BlockSpec((tt, c), lambda i: (i, 0)),
                  pl.BlockSpec((8, c), lambda i: (jnp.minimum((i + 1) * (tt // 8), n // 8 - 1), 0)),
                  pl.BlockSpec((8, c), lambda i: (0, 0)),
                  pl.BlockSpec((1, c), lambda i: (0, 0))],
        out_specs=pl.BlockSpec((tt, c), lambda i: (i, 0)),
        scratch_shapes=[pltpu.VMEM((tt + 16, c), F32)],
        compiler_params=_cparams(("parallel",)),
    )(qk_raw, qk_raw, qk_raw, w_pad, conv_b.reshape(1, c))


def _log_sigmoid(x):
    return jnp.minimum(x, 0.0) - jnp.log(1.0 + jnp.exp(-jnp.abs(x)))


def _mlstm_direction(q_ref, k_ref, v_ref, g_ref, bias_ref, h_ref, c_scr, n_scr, m_scr, *, reverse):
    L = ML_CHUNK
    row = lax.broadcasted_iota(I32, (L, L), 0)
    col = lax.broadcasted_iota(I32, (L, L), 1)
    causal = (col >= row) if reverse else (col <= row)
    gates = g_ref[...] + bias_ref[...]
    logf = _log_sigmoid(gates)
    b_cols = jnp.dot(causal.astype(F32), logf, preferred_element_type=F32,
                     precision=lax.Precision.HIGHEST)
    gates_t = gates.T
    b_rows = b_cols.T
    edge = 0 if reverse else L - 1
    off = 2 * ML_HEADS if reverse else 0
    for h in range(ML_HEADS):
        st = (ML_HEADS if reverse else 0) + h
        ci = off + h
        cf = off + ML_HEADS + h
        q = q_ref[:, h * HEAD_DIM:(h + 1) * HEAD_DIM]
        k = k_ref[:, h * HEAD_DIM:(h + 1) * HEAD_DIM]
        v = v_ref[:, h * ML_V_DIM:(h + 1) * ML_V_DIM]
        ib_r = gates_t[ci:ci + 1, :]
        ib_c = gates[:, ci:ci + 1]
        b_r = b_rows[cf:cf + 1, :]
        b_c = b_cols[:, cf:cf + 1]
        b_last = b_cols[edge:edge + 1, cf:cf + 1]
        m_prev = m_scr[st][:, 0:1]
        c_prev = c_scr[st]
        n_prev = n_scr[st]
        dlog = jnp.where(causal, b_c - b_r + ib_r, MASK_VALUE)
        inter = b_c + m_prev
        m_i = jnp.maximum(inter, jnp.max(dlog, axis=-1, keepdims=True))
        w = jnp.exp(dlog - m_i)
        s = lax.dot_general(q, k, (((1,), (1,)), ((), ())), preferred_element_type=F32) * w
        sc = jnp.exp(inter - m_i)
        num = jnp.dot(s.astype(BF16), v, preferred_element_type=F32)
        num = num + sc * jnp.dot(q, c_prev.astype(BF16), preferred_element_type=F32)
        qn = jnp.sum(q.astype(F32) * n_prev, axis=-1, keepdims=True)
        den = jnp.sum(s, axis=-1, keepdims=True) + sc * qn
        h_ref[:, h * ML_V_DIM:(h + 1) * ML_V_DIM] = num / jnp.maximum(jnp.abs(den), jnp.exp(-m_i))
        g_r = b_last - b_r + ib_r
        g_c = b_last - b_c + ib_c
        m_new = jnp.maximum(b_last + m_prev, jnp.max(g_r, axis=-1, keepdims=True))
        decay = jnp.exp(b_last + m_prev - m_new)
        kw = k.astype(F32) * jnp.exp(g_c - m_new)
        c_scr[st] = decay * c_prev + lax.dot_general(kw.astype(BF16), v, (((0,), (0,)), ((), ())),
                                                     preferred_element_type=F32)
        n_scr[st] = decay * n_prev + jnp.sum(kw, axis=0, keepdims=True)
        m_scr[st] = jnp.broadcast_to(m_new, (1, LANES))


def _mlstm_kernel(qf_ref, kf_ref, vf_ref, gf_ref, qb_ref, kb_ref, vb_ref, gb_ref, bias_ref,
                  hf_ref, hb_ref, c_scr, n_scr, m_scr):
    @pl.when(pl.program_id(1) == 0)
    def _():
        c_scr[...] = jnp.zeros_like(c_scr)
        n_scr[...] = jnp.zeros_like(n_scr)
        m_scr[...] = jnp.zeros_like(m_scr)

    _mlstm_direction(qf_ref, kf_ref, vf_ref, gf_ref, bias_ref, hf_ref, c_scr, n_scr, m_scr, reverse=False)
    _mlstm_direction(qb_ref, kb_ref, vb_ref, gb_ref, bias_ref, hb_ref, c_scr, n_scr, m_scr, reverse=True)


def mlstm_scan(qk, v, gates, bias_pad, *, batch, seq):
    n = batch * seq
    nc = seq // ML_CHUNK
    fwd = lambda col: (lambda b, c: (b * nc + c, col))
    bwd = lambda col: (lambda b, c: (b * nc + nc - 1 - c, col))

    def specs(at):
        return [pl.BlockSpec((ML_CHUNK, ML_QK_W), at(0)),
                pl.BlockSpec((ML_CHUNK, ML_QK_W), at(1)),
                pl.BlockSpec((ML_CHUNK, ML_V_W), at(0)),
                pl.BlockSpec((ML_CHUNK, GATE_PAD), at(0))]

    out = jax.ShapeDtypeStruct((n, ML_V_W), F32)
    return pl.pallas_call(
        _mlstm_kernel,
        out_shape=[out, out],
        grid=(batch, nc),
        in_specs=specs(fwd) + specs(bwd) + [pl.BlockSpec((1, GATE_PAD), lambda b, c: (0, 0))],
        out_specs=[pl.BlockSpec((ML_CHUNK, ML_V_W), fwd(0)), pl.BlockSpec((ML_CHUNK, ML_V_W), bwd(0))],
        scratch_shapes=[pltpu.VMEM((2 * ML_HEADS, HEAD_DIM, ML_V_DIM), F32),
                        pltpu.VMEM((2 * ML_HEADS, 1, HEAD_DIM), F32),
                        pltpu.VMEM((2 * ML_HEADS, 1, LANES), F32)],
        compiler_params=_cparams(("parallel", "arbitrary")),
    )(qk, qk, v, gates, qk, qk, v, gates, bias_pad)


def _postmix_kernel(att_ref, hf_ref, hb_ref, mlo_ref, mg_ref, x_ref, g1_ref, sc2_ref, sh2_ref,
                    n2w_ref, mlnw_ref, wml_ref, watt_ref, wo_ref, wrh_ref, wrl_ref,
                    x1_ref, h2_ref, lg_ref, *, d):
    hm = hf_ref[...] + hb_ref[...]
    parts = []
    for h in range(ML_HEADS):
        hh = hm[:, h * ML_V_DIM:(h + 1) * ML_V_DIM]
        parts.append(hh * lax.rsqrt(jnp.mean(hh * hh, axis=-1, keepdims=True) + NORM_EPS))
    hn = jnp.concatenate(parts, axis=1) * mlnw_ref[...]
    ml_act = (_sigmoid(mlo_ref[...].astype(F32)) * hn).astype(BF16)
    y_ml = jnp.dot(ml_act, wml_ref[...], preferred_element_type=F32)
    y_att = jnp.dot(att_ref[...], watt_ref[...], preferred_element_type=F32)
    g_ml = _sigmoid(mg_ref[:, 0:d].astype(F32))
    g_att = _sigmoid(mg_ref[:, d:2 * d].astype(F32))
    merged = (g_ml * y_ml + g_att * y_att).astype(BF16)
    out = jnp.dot(merged, wo_ref[...], preferred_element_type=F32)
    x1 = x_ref[...] + g1_ref[0] * out
    x1_ref[...] = x1
    ms = jnp.mean(x1 * x1, axis=-1, keepdims=True)
    h2 = x1 * lax.rsqrt(ms + NORM_EPS) * n2w_ref[...]
    h2 = h2 * (1.0 + sc2_ref[0]) + sh2_ref[0]
    h2_ref[...] = _pack_pair(h2[:, :d // 2], h2[:, d // 2:])
    h_hi = h2.astype(BF16)
    h_lo = (h2 - h_hi.astype(F32)).astype(BF16)
    lg_ref[...] = (jnp.dot(h_hi, wrh_ref[...], preferred_element_type=F32)
                   + jnp.dot(h_lo, wrh_ref[...], preferred_element_type=F32)
                   + jnp.dot(h_hi, wrl_ref[...], preferred_element_type=F32))


def post_mixer(att_o, h_f, h_b, ml_o, merge, x2d, gate1, scale2, shift2, norm2_w, ml_norm_w,
               w_ml_out, w_att_out, w_o, w_router_hi, w_router_lo, *, seq):
    n, d = x2d.shape
    tm = 256
    tpb = seq // tm
    const = lambda shape: pl.BlockSpec(shape, lambda i: tuple(0 for _ in shape), pipeline_mode=pl.Buffered(1))
    rows = lambda w: pl.BlockSpec((tm, w), lambda i: (i, 0))
    per_b = pl.BlockSpec((1, 1, d), lambda i: (i // tpb, 0, 0))
    return pl.pallas_call(
        functools.partial(_postmix_kernel, d=d),
        out_shape=[jax.ShapeDtypeStruct((n, d), F32),
                   jax.ShapeDtypeStruct((n, d // 2), I32),
                   jax.ShapeDtypeStruct((n, LANES), F32)],
        grid=(n // tm,),
        in_specs=[rows(ATT_GROUP_W), rows(ML_V_W), rows(ML_V_W), rows(ML_V_W), rows(2 * d), rows(d),
                  per_b, per_b, per_b, const((1, d)), const((1, ML_V_W)),
                  const((ML_V_W, d)), const((ATT_GROUP_W, d)), const((d, d)), const((d, LANES)), const((d, LANES))],
        out_specs=[rows(d), rows(d // 2), rows(LANES)],
        compiler_params=_cparams(("parallel",)),
    )(att_o, h_f, h_b, ml_o, merge, x2d, gate1, scale2, shift2, norm2_w, ml_norm_w,
      w_ml_out, w_att_out, w_o, w_router_hi, w_router_lo)


def _split3(x):
    p0 = x.astype(BF16)
    r1 = x - p0.astype(F32)
    p1 = r1.astype(BF16)
    p2 = (r1 - p1.astype(F32)).astype(BF16)
    return p0, p1, p2


def _route_select_kernel(lg_ref, gate_ref, dest_ref, lcs_ref, bp_ref, bs_ref, off_ref, cnt_ref, *, cap, n_tok):
    lg = lg_ref[...]
    e, nb, _ = lg.shape
    mx = jnp.max(lg, axis=0, keepdims=True)
    ex = jnp.exp(lg - mx)
    aff = ex / jnp.sum(ex, axis=0, keepdims=True)
    bits = lax.bitcast_convert_type(aff, I32)

    def count(pred):
        c = jnp.sum(jnp.where(pred, 1.0, 0.0), axis=1, keepdims=True)
        return jnp.sum(c, axis=2, keepdims=True)

    capf = jnp.float32(cap)

    def thr_step(_, c):
        lo, hi = c
        mid = lo + lax.shift_right_logical(hi - lo + 1, 1)
        ok = count(bits >= mid) >= capf
        return jnp.where(ok, mid, lo), jnp.where(ok, hi, mid - 1)

    lo0 = jnp.zeros((e, 1, 1), I32)
    hi0 = jnp.full((e, 1, 1), 0x7F800000, I32)
    thr, _ = lax.fori_loop(0, 31, thr_step, (lo0, hi0))
    gt = bits > thr
    eq = bits == thr
    need = capf - count(gt)
    nidx = (lax.broadcasted_iota(I32, (1, nb, LANES), 1) * LANES + lax.broadcasted_iota(I32, (1, nb, LANES), 2))

    def tie_step(_, c):
        lo, hi = c
        mid = lax.shift_right_logical(lo + hi, 1)
        ok = count(eq & (nidx < mid)) >= need
        return jnp.where(ok, lo, mid), jnp.where(ok, mid, hi)

    n_iter = max(1, (n_tok - 1).bit_length() + 1)
    _, jcut = lax.fori_loop(0, n_iter, tie_step, (jnp.zeros((e, 1, 1), I32), jnp.full((e, 1, 1), n_tok, I32)))
    mask = jnp.where(gt | (eq & (nidx < jcut)), 1.0, 0.0)
    gate_ref[...] = aff * mask

    jr = lax.broadcasted_iota(I32, (LANES, LANES), 0)
    jc = lax.broadcasted_iota(I32, (LANES, LANES), 1)
    tri_incl = jnp.where(jr <= jc, 1.0, 0.0).astype(BF16)
    br = lax.broadcasted_iota(I32, (nb, nb), 0)
    bc = lax.broadcasted_iota(I32, (nb, nb), 1)
    tri_prev = jnp.where(bc < br, 1.0, 0.0).astype(BF16)

    lcs = jnp.dot(mask.reshape(e * nb, LANES).astype(BF16), tri_incl, preferred_element_type=F32)
    lcs = lcs.reshape(e, nb, LANES)
    lcs_ref[...] = lcs.astype(BF16)
    cnt = jnp.zeros((nb, LANES), F32)
    ranks = []
    for ei in range(e):
        bs = jnp.broadcast_to(lcs[ei][:, LANES - 1:LANES], (nb, LANES))
        bs_ref[ei] = bs
        bp_ref[ei] = jnp.dot(tri_prev, bs.astype(BF16), preferred_element_type=F32)
        ranks.append(cnt)
        cnt = cnt + mask[ei]
    lcs_k = jnp.dot(cnt.astype(BF16), tri_incl, preferred_element_type=F32)
    bs_k = jnp.broadcast_to(lcs_k[:, LANES - 1:LANES], (nb, LANES))
    bs_hi = jnp.floor(bs_k * (1.0 / 64.0))
    bs_lo = bs_k - 64.0 * bs_hi
    bp_k = (64.0 * jnp.dot(tri_prev, bs_hi.astype(BF16), preferred_element_type=F32)
            + jnp.dot(tri_prev, bs_lo.astype(BF16), preferred_element_type=F32))
    off = bp_k + lcs_k - cnt
    off_ref[...] = off.astype(I32)
    cnt_ref[...] = cnt.astype(I32)
    for ei in range(e):
        dest_ref[ei] = jnp.where(mask[ei] > 0.0, off + ranks[ei], -1.0)


def _route_compact_kernel(lcs_ref, bp_ref, bs_ref, dest_ref, gate_ref, idx_out, dst_out, gate_out, *, cap):
    lcs = lcs_ref[0]
    bp = bp_ref[0]
    bs = bs_ref[0]
    nb = lcs.shape[0]
    blk_id = lax.broadcasted_iota(I32, (nb, LANES), 0).astype(F32)
    lane = lax.broadcasted_iota(I32, (1, LANES), 1).astype(F32)
    oh_parts, base_parts, blk_parts = [], [], []
    for k in range(cap // LANES):
        r = lane + float(k * LANES)
        hit = (bp <= r) & (r < bp + bs)
        oh_parts.append(jnp.where(hit, 1.0, 0.0).astype(BF16))
        base_parts.append(jnp.sum(jnp.where(hit, bp, 0.0), axis=0, keepdims=True))
        blk_parts.append(jnp.sum(jnp.where(hit, blk_id, 0.0), axis=0, keepdims=True))
    onehot = jnp.concatenate(oh_parts, axis=1)
    base = jnp.concatenate(base_parts, axis=1)
    blk = jnp.concatenate(blk_parts, axis=1)
    rows = lax.broadcasted_iota(I32, (1, cap), 1).astype(F32)
    contract0 = (((0,), (0,)), ((), ()))
    counts = lax.dot_general(lcs, onehot, contract0, preferred_element_type=F32)
    jloc = jnp.sum(jnp.where(counts <= rows - base, 1.0, 0.0), axis=0, keepdims=True)
    dvals = jnp.zeros((LANES, cap), F32)
    for piece in _split3(dest_ref[0]):
        dvals = dvals + lax.dot_general(piece, onehot, contract0, preferred_element_type=F32)
    jcol = lax.broadcasted_iota(I32, (LANES, cap), 0).astype(F32)
    dst = jnp.sum(jnp.where(jcol == jloc, dvals, 0.0), axis=0, keepdims=True)
    gvals = jnp.zeros((LANES, cap), F32)
    for piece in _split3(gate_ref[0]):
        gvals = gvals + lax.dot_general(piece, onehot, contract0, preferred_element_type=F32)
    idx_out[0] = (blk * float(LANES) + jloc).astype(I32)
    dst_out[0] = dst.astype(I32)
    gate_out[0] = jnp.sum(jnp.where(jcol == jloc, gvals, 0.0), axis=0, keepdims=True)


def route_tokens(logits, n_experts):
    n = logits.shape[0]
    e = n_experts
    nb = n // LANES
    cap = CAPACITY_FACTOR * n // e
    assert n % LANES == 0 and cap % LANES == 0
    lg = logits[:, :e].T.reshape(e, nb, LANES)
    big = lambda dt: jax.ShapeDtypeStruct((e, nb, LANES), dt)
    small = jax.ShapeDtypeStruct((nb, LANES), I32)
    gate, dest, lcs, bp, bs, off, cnt = pl.pallas_call(
        functools.partial(_route_select_kernel, cap=cap, n_tok=n),
        out_shape=[big(F32), big(F32), big(BF16), big(F32), big(F32), small, small],
        compiler_params=pltpu.CompilerParams(vmem_limit_bytes=VMEM_LIMIT),
    )(lg)
    per_e = lambda: pl.BlockSpec((1, nb, LANES), lambda i: (i, 0, 0))
    slot = lambda: pl.BlockSpec((1, 1, cap), lambda i: (i, 0, 0))
    idx, dst, gate_slot = pl.pallas_call(
        functools.partial(_route_compact_kernel, cap=cap),
        out_shape=[jax.ShapeDtypeStruct((e, 1, cap), I32), jax.ShapeDtypeStruct((e, 1, cap), I32),
                   jax.ShapeDtypeStruct((e, 1, cap), F32)],
        grid=(e,),
        in_specs=[per_e(), per_e(), per_e(), per_e(), per_e()],
        out_specs=[slot(), slot(), slot()],
        compiler_params=_cparams(("parallel",)),
    )(lcs, bp, bs, dest, gate)
    return idx.reshape(-1), dst.reshape(-1), gate_slot.reshape(-1), off.reshape(-1), cnt.reshape(-1)


def _expert_kernel(idx_ref, idxn_ref, dst_ref, gate_ref, wg_ref, wu_ref, wd_ref, h2_hbm, z_hbm,
                   xbuf, zbuf, xs, acc, gsem, ssem, *, d, ff, tr):
    s = pl.program_id(0)
    nsteps = pl.num_programs(0)
    half = d // 2

    def start_gather(ids, rows=range(tr)):
        for r in rows:
            pltpu.make_async_copy(h2_hbm.at[pl.ds(ids[0, 0, r], 1), :], xbuf.at[pl.ds(r, 1), :], gsem.at[0]).start()

    def wait_gather():
        pltpu.make_async_copy(h2_hbm.at[pl.ds(0, tr), :], xbuf, gsem.at[0]).wait()

    def wait_scatter():
        pltpu.make_async_copy(zbuf, z_hbm.at[pl.ds(0, tr), :], ssem.at[0]).wait()

    @pl.when(s == 0)
    def _():
        start_gather(idx_ref)

    wait_gather()
    lo, hi = _unpack_pair(xbuf[...])
    xs[:, :half] = lo.astype(BF16)
    xs[:, half:] = hi.astype(BF16)
    x = xs[...]
    tf = min(ff, 512)
    n_groups = 3 * (ff // tf)
    per_group = -(-tr // n_groups)
    groups = [range(i * per_group, min((i + 1) * per_group, tr)) for i in range(n_groups)]
    for f in range(ff // tf):
        g = jnp.dot(x, wg_ref[0, :, f * tf:(f + 1) * tf], preferred_element_type=F32)
        start_gather(idxn_ref, groups[3 * f])
        u = jnp.dot(x, wu_ref[0, :, f * tf:(f + 1) * tf], preferred_element_type=F32)
        start_gather(idxn_ref, groups[3 * f + 1])
        hid = (_silu(g) * u).astype(BF16)
        part = jnp.dot(hid, wd_ref[0, f * tf:(f + 1) * tf, :], preferred_element_type=F32)
        start_gather(idxn_ref, groups[3 * f + 2])
        if f == 0:
            acc[...] = part
        else:
            acc[...] += part

    @pl.when(s > 0)
    def _():
        wait_scatter()

    gcol = jnp.broadcast_to(gate_ref[0], (LANES, tr)).T
    gwide = jnp.concatenate([gcol] * (half // LANES), axis=1)
    zbuf[...] = _pack_pair(acc[:, :half] * gwide, acc[:, half:] * gwide)
    for r in range(tr):
        pltpu.make_async_copy(zbuf.at[pl.ds(r, 1), :], z_hbm.at[pl.ds(dst_ref[0, 0, r], 1), :], ssem.at[0]).start()

    @pl.when(s == nsteps - 1)
    def _():
        wait_scatter()
        wait_gather()


def expert_ffn(h2_packed, idx, dest, gate, w_gate, w_up, w_down):
    n, half = h2_packed.shape
    e, d, ff = w_gate.shape
    total = idx.shape[0]
    cap = total // e
    tr = min(EXPERT_ROWS, cap)
    nsteps = total // tr
    per_e = cap // tr
    idx3 = idx.reshape(nsteps, 1, tr)
    dst3 = dest.reshape(nsteps, 1, tr)
    gate3 = gate.reshape(nsteps, 1, tr)
    smem = lambda f: pl.BlockSpec((1, 1, tr), f, memory_space=pltpu.SMEM)
    wspec = lambda shape, nbuf=1: pl.BlockSpec(shape, lambda s: (s // per_e, 0, 0), pipeline_mode=pl.Buffered(nbuf))
    return pl.pallas_call(
        functools.partial(_expert_kernel, d=d, ff=ff, tr=tr),
        out_shape=jax.ShapeDtypeStruct((total, half), I32),
        grid=(nsteps,),
        in_specs=[smem(lambda s: (s, 0, 0)),
                  smem(lambda s: (jnp.minimum(s + 1, nsteps - 1), 0, 0)),
                  smem(lambda s: (s, 0, 0)),
                  pl.BlockSpec((1, 1, tr), lambda s: (s, 0, 0)),
                  wspec((1, d, ff), 2), wspec((1, d, ff)), wspec((1, ff, d)),
                  pl.BlockSpec(memory_space=pl.ANY)],
        out_specs=pl.BlockSpec(memory_space=pl.ANY),
        scratch_shapes=[pltpu.VMEM((tr, half), I32),
                        pltpu.VMEM((tr, half), I32),
                        pltpu.VMEM((tr, d), BF16),
                        pltpu.VMEM((tr, d), F32),
                        pltpu.SemaphoreType.DMA((1,)),
                        pltpu.SemaphoreType.DMA((1,))],
        compiler_params=_cparams(("arbitrary",)),
    )(idx3, idx3, dst3, gate3, w_gate, w_up, w_down, h2_packed)


def _combine_kernel(tile_ref, chunk_ref, flag_ref, z_ref, oc_ref, x_ref, g2_ref, fw_ref, o_ref,
                    acc, *, d, tm, tr, final):
    w = pl.program_id(0)
    flags = flag_ref[w]

    @pl.when((flags & 2) != 0)
    def _():
        acc[...] = jnp.zeros_like(acc)

    @pl.when((flags & 1) != 0)
    def _():
        lo, hi = _unpack_pair(z_ref[...])
        zs = jnp.concatenate([lo.astype(BF16), hi.astype(BF16)], axis=1)
        zrow = lax.broadcasted_iota(I32, (tm, tr), 1) + chunk_ref[w] * tr
        oc = oc_ref[...]
        off = oc[:, 0:1]
        cnt = oc[:, 1:2]
        sel = jnp.where((zrow >= off) & (zrow < off + cnt), 1.0, 0.0).astype(BF16)
        acc[...] += jnp.dot(sel, zs, preferred_element_type=F32)

    @pl.when((flags & 4) != 0)
    def _():
        x2 = x_ref[...] + g2_ref[0] * acc[...]
        if final:
            ms = jnp.mean(x2 * x2, axis=-1, keepdims=True)
            x2 = x2 * lax.rsqrt(ms + NORM_EPS) * fw_ref[...]
        o_ref[...] = x2


def moe_combine(z_sorted, item_tile, item_chunk, item_flags, off_cnt, x1, gate2, final_w, *, seq, final):
    n, d = x1.shape
    total, half = z_sorted.shape
    tm = min(MOE_ROWS, n)
    tr = min(MOE_ROWS, total)
    tpb = seq // tm
    n_items = item_tile.shape[0]
    return pl.pallas_call(
        functools.partial(_combine_kernel, d=d, tm=tm, tr=tr, final=final),
        out_shape=jax.ShapeDtypeStruct((n, d), F32),
        grid_spec=pltpu.PrefetchScalarGridSpec(
            num_scalar_prefetch=3,
            grid=(n_items,),
            in_specs=[pl.BlockSpec((tr, half), lambda w, t, c, f: (c[w], 0)),
                      pl.BlockSpec((tm, 2), lambda w, t, c, f: (t[w], 0)),
                      pl.BlockSpec((tm, d), lambda w, t, c, f: (t[w], 0)),
                      pl.BlockSpec((1, 1, d), lambda w, t, c, f: (t[w] // tpb, 0, 0)),
                      pl.BlockSpec((1, d), lambda w, t, c, f: (0, 0))],
            out_specs=pl.BlockSpec((tm, d), lambda w, t, c, f: (t[w], 0)),
            scratch_shapes=[pltpu.VMEM((tm, d), F32)]),
        compiler_params=_cparams(("arbitrary",)),
    )(item_tile, item_chunk, item_flags, z_sorted, off_cnt, x1, gate2, final_w)


def _combine_items(off, n, total, tm, tr):
    nt = n // tm
    nchunks = total // tr
    first = off.reshape(nt, tm)[:, 0]
    end = jnp.concatenate([first[1:], jnp.array([total], I32)])
    c_lo = jnp.minimum(first // tr, nchunks - 1)
    c_hi = jnp.where(end > first, (end - 1) // tr, c_lo)
    cnt = c_hi - c_lo + 1
    start = jnp.cumsum(cnt) - cnt
    n_items = nt + nchunks
    w = jnp.arange(n_items, dtype=I32)
    t = jnp.clip(jnp.searchsorted(start, w, side="right").astype(I32) - 1, 0, nt - 1)
    k = w - start[t]
    valid = k < cnt[t]
    k = jnp.minimum(k, cnt[t] - 1)
    flags = jnp.where(valid, 1 + 2 * (k == 0) + 4 * (k == cnt[t] - 1), 0)
    return t.astype(I32), (c_lo[t] + k).astype(I32), flags.astype(I32)


def _rope_tables(seq):
    freqs = jnp.float32(ROT_THETA) ** (-jnp.arange(ROT_HALF, dtype=F32) / ROT_HALF)
    ang = jnp.arange(seq, dtype=F32)[:, None] * freqs[None, :]
    cos = jnp.cos(ang)
    sin = jnp.sin(ang)
    ones = jnp.ones((seq, HEAD_DIM - 2 * ROT_HALF), F32)
    cos_t = jnp.concatenate([cos, cos, ones], axis=1)
    sin_t = jnp.concatenate([-sin, sin, 0.0 * ones], axis=1)
    return cos_t, sin_t


def _prep_layer(p, l):
    w_in = p["w_in"][l]
    aw = 3 * ATT_GROUP_W
    q, k, v = w_in[:, 0:aw], w_in[:, aw:2 * aw], w_in[:, 2 * aw:3 * aw]
    cols = []
    for g in range(3):
        sl = slice(g * ATT_GROUP_W, (g + 1) * ATT_GROUP_W)
        cols += [q[:, sl], k[:, sl], v[:, sl]]
    o = 3 * aw
    cols.append(w_in[:, o:o + 2 * ML_QK_W + 2 * ML_V_W])
    o += 2 * ML_QK_W + 2 * ML_V_W
    w_gates = w_in[:, o:o + 4 * ML_HEADS]
    o += 4 * ML_HEADS
    cols.append(w_in[:, o:])
    d = w_in.shape[0]
    w_router = jnp.zeros((d, LANES), F32).at[:, :N_EXPERTS].set(p["w_router"][l])
    return dict(
        w_main=jnp.stack([t for c in cols for t in jnp.split(c, c.shape[1] // PROJ_TN, axis=1)]).astype(BF16),
        w_gates=jnp.zeros((d, GATE_PAD), BF16).at[:, :4 * ML_HEADS].set(w_gates.astype(BF16)),
        bias_pad=jnp.zeros((1, GATE_PAD), F32).at[0, :4 * ML_HEADS].set(p["b_mgate"][l]),
        w_ml_out=p["w_ml_out"][l].astype(BF16),
        w_att_out=p["w_att_out"][l].astype(BF16),
        w_o=p["w_o"][l].astype(BF16),
        w_router_hi=w_router.astype(BF16),
        w_router_lo=(w_router - w_router.astype(BF16).astype(F32)).astype(BF16),
        w_exp_gate=p["w_exp_gate"][l].astype(BF16),
        w_exp_up=p["w_exp_up"][l].astype(BF16),
        w_exp_down=p["w_exp_down"][l].astype(BF16),
    )


def _moe_sublayer(x1, h2p, logits, gate2, lw, final_w, *, seq, final):
    n = x1.shape[0]
    idx, dest, gate, off, cnt = route_tokens(logits, N_EXPERTS)
    z = expert_ffn(h2p, idx, dest, gate, lw["w_exp_gate"], lw["w_exp_up"], lw["w_exp_down"])
    tm = min(MOE_ROWS, n)
    tr = min(MOE_ROWS, z.shape[0])
    items = _combine_items(off, n, z.shape[0], tm, tr)
    return moe_combine(z, *items, jnp.stack([off, cnt], axis=1), x1, gate2, final_w, seq=seq, final=final)


def _run_group(x, mods, layers, p, cos_t, sin_t):
    batch, seq, d = x.shape
    n = batch * seq
    x2d = x.reshape(n, d)
    depth = len(layers)
    for l, lw in enumerate(layers):
        shift1, scale1, gate1, shift2, scale2, gate2 = [m.reshape(batch, 1, d) for m in jnp.split(mods[l], 6, axis=-1)]
        a0, a1, a2, ml_qk, ml_v, ml_o, merge, gates = input_projection(
            x2d, scale1, shift1, p["norm1_w"][l].reshape(1, d), cos_t, sin_t, lw["w_main"], lw["w_gates"],
            batch=batch, seq=seq)
        att_o = attention_branch((a0, a1, a2), batch=batch, seq=seq)
        qk = mlstm_conv(ml_qk, p["conv_w"][l], p["conv_b"][l], seq=seq)
        h_f, h_b = mlstm_scan(qk, ml_v, gates, lw["bias_pad"], batch=batch, seq=seq)
        x1, h2p, logits = post_mixer(att_o, h_f, h_b, ml_o, merge, x2d, gate1, scale2, shift2,
                                     p["norm2_w"][l].reshape(1, d), p["ml_norm_w"][l].reshape(1, ML_V_W),
                                     lw["w_ml_out"], lw["w_att_out"], lw["w_o"], lw["w_router_hi"], lw["w_router_lo"], seq=seq)
        x2d = _moe_sublayer(x1, h2p, logits, gate2, lw, p["final_norm_w"].reshape(1, d),
                            seq=seq, final=(l == depth - 1))
    return x2d.reshape(batch, seq, d)


def kernel(x_prompt, x_sample, c_prompt, c_sample, norm1_w, norm2_w, w_ada, b_ada, w_in, b_mgate, conv_w, conv_b, ml_norm_w, w_ml_out, w_att_out, w_o, w_router, w_exp_gate, w_exp_up, w_exp_down, final_norm_w):
    p = dict(norm1_w=norm1_w, norm2_w=norm2_w, w_in=w_in, b_mgate=b_mgate, conv_w=conv_w, conv_b=conv_b,
             ml_norm_w=ml_norm_w, w_ml_out=w_ml_out, w_att_out=w_att_out, w_o=w_o, w_router=w_router,
             w_exp_gate=w_exp_gate, w_exp_up=w_exp_up, w_exp_down=w_exp_down, final_norm_w=final_norm_w)
    depth = w_in.shape[0]
    bp, bs = c_prompt.shape[0], c_sample.shape[0]
    m_pad = -(-(bp + bs) // 8) * 8
    c_pad = jnp.zeros((m_pad, c_prompt.shape[1]), F32).at[:bp].set(c_prompt).at[bp:bp + bs].set(c_sample)
    mods = ada_modulation(c_pad, w_ada, b_ada)
    layers = [_prep_layer(p, l) for l in range(depth)]
    outs = []
    for x, rows in ((x_prompt, slice(0, bp)), (x_sample, slice(bp, bp + bs))):
        cos_t, sin_t = _rope_tables(x.shape[1])
        outs.append(_run_group(x, mods[:, rows], layers, p, cos_t, sin_t))
    return tuple(outs)
```

```python
import functools
import math

import jax
import jax.numpy as jnp
from jax import lax
from jax.experimental import pallas as pl
from jax.experimental.pallas import tpu as pltpu

F32 = jnp.float32
BF16 = jnp.bfloat16
I32 = jnp.int32

NORM_EPS = 1e-6
MASK_VALUE = -1e30
HEAD_DIM = 128
ATT_DILATIONS = (1, 4, 16)
ATT_WINDOWS = (128, 512, 2048)
ATT_HALF = 64
ATT_HEADS_PER_GROUP = 4
ATT_GROUP_W = ATT_HEADS_PER_GROUP * HEAD_DIM
ROT_HALF = 16
ROT_THETA = 500000.0
ML_HEADS = 4
ML_V_DIM = 256
ML_QK_W = ML_HEADS * HEAD_DIM
ML_V_W = ML_HEADS * ML_V_DIM
ML_CHUNK = 128
ML_CONV_TAPS = 5
N_EXPERTS = 16
CAPACITY_FACTOR = 2
GATE_PAD = 128
LANES = 128
VMEM_LIMIT = 56 * 1024 * 1024
QK_SCALE = HEAD_DIM ** -0.5

PROJ_TN = 512
PROJ_ROWS = 256
ATT_POS_TILE = 2048
MOE_ROWS = 256
EXPERT_ROWS = 512


def _cparams(sem):
    return pltpu.CompilerParams(dimension_semantics=sem, vmem_limit_bytes=VMEM_LIMIT)


def _sigmoid(x):
    return 1.0 / (1.0 + jnp.exp(-x))


def _silu(x):
    return x * _sigmoid(x)


def _bf16_bits(x):
    b = lax.bitcast_convert_type(x, I32)
    return b + 0x7FFF + (lax.shift_right_logical(b, 16) & 1)


def _pack_pair(lo, hi):
    return (_bf16_bits(hi) & (-65536)) | lax.shift_right_logical(_bf16_bits(lo), 16)


def _unpack_pair(w):
    lo = lax.bitcast_convert_type(lax.shift_left(w, 16), F32)
    hi = lax.bitcast_convert_type(w & (-65536), F32)
    return lo, hi


def _ada_kernel(c_ref, w_ref, b_ref, o_ref):
    c = c_ref[...]
    a = _silu(c).astype(BF16)
    o_ref[0] = jnp.dot(a, w_ref[0].astype(BF16), preferred_element_type=F32) + b_ref[0]


def ada_modulation(c_pad, w_ada, b_ada):
    m, d = c_pad.shape
    depth, _, n = w_ada.shape
    tn = min(n, 1536)
    return pl.pallas_call(
        _ada_kernel,
        out_shape=jax.ShapeDtypeStruct((depth, m, n), F32),
        grid=(depth, n // tn),
        in_specs=[pl.BlockSpec((m, d), lambda l, j: (0, 0)),
                  pl.BlockSpec((1, d, tn), lambda l, j: (l, 0, j)),
                  pl.BlockSpec((1, 1, tn), lambda l, j: (l, 0, j))],
        out_specs=pl.BlockSpec((1, m, tn), lambda l, j: (l, 0, j)),
        compiler_params=_cparams(("parallel", "parallel")),
    )(c_pad, w_ada, b_ada.reshape(depth, 1, n))


def _rope(a, cos, sin_signed):
    up = pltpu.roll(a, HEAD_DIM - ROT_HALF, axis=1)
    dn = pltpu.roll(a, ROT_HALF, axis=1)
    lane = lax.broadcasted_iota(I32, a.shape, 1)
    partner = jnp.where(lane < ROT_HALF, up, dn)
    return a * cos + partner * sin_signed


def _inproj_kernel(x_ref, sc_ref, sh_ref, nw_ref, cos_ref, sin_ref, w_ref, wg_ref,
                   a0_ref, a1_ref, a2_ref, qk_ref, v_ref, o_ref, mg_ref, gt_ref,
                   h_scr, acc_scr, *, tm):
    j = pl.program_id(1)

    @pl.when(j == 0)
    def _():
        x = x_ref[...]
        ms = jnp.mean(x * x, axis=-1, keepdims=True)
        h = x * lax.rsqrt(ms + NORM_EPS) * nw_ref[...]
        h = h * (1.0 + sc_ref[0]) + sh_ref[0]
        hb = h.astype(BF16)
        h_scr[...] = hb
        gt_ref[...] = jnp.dot(hb, wg_ref[...], preferred_element_type=F32)

    row_chunks = [(r0, min(PROJ_ROWS, tm - r0)) for r0 in range(0, tm, PROJ_ROWS)]

    def project(r0, nr):
        return jnp.dot(h_scr[r0:r0 + nr, :], w_ref[0], preferred_element_type=F32)

    att_refs = (a0_ref, a1_ref, a2_ref)
    for g, dil in enumerate(ATT_DILATIONS):
        for t in range(3):

            @pl.when(j == 3 * g + t)
            def _(g=g, dil=dil, t=t):
                out = att_refs[g]
                for r0, nr in row_chunks:
                    a = project(r0, nr)
                    if t < 2:
                        cos = cos_ref[r0:r0 + nr, :]
                        sin = sin_ref[r0:r0 + nr, :]
                        heads = []
                        for hh in range(ATT_HEADS_PER_GROUP):
                            ah = a[:, hh * HEAD_DIM:(hh + 1) * HEAD_DIM]
                            if t == 0:
                                ah = ah * QK_SCALE
                            heads.append(_rope(ah, cos, sin))
                        a = jnp.concatenate(heads, axis=1)
                    if dil == 1:
                        out[0, 0, r0:r0 + nr, :] = a.astype(BF16)
                    else:
                        for hh in range(ATT_HEADS_PER_GROUP):
                            acc_scr[hh, r0:r0 + nr, :] = a[:, hh * HEAD_DIM:(hh + 1) * HEAD_DIM]
                        for rho in range(dil):
                            for hh in range(ATT_HEADS_PER_GROUP):
                                out[0, rho, r0 // dil:(r0 + nr) // dil, hh * HEAD_DIM:(hh + 1) * HEAD_DIM] = (
                                    acc_scr[hh, pl.ds(r0 + rho, nr // dil, stride=dil), :].astype(BF16))

    @pl.when((j == 9) | (j == 10))
    def _():
        for r0, nr in row_chunks:
            qk_ref[r0:r0 + nr, :] = project(r0, nr)

    @pl.when((j == 11) | (j == 12))
    def _():
        for r0, nr in row_chunks:
            v_ref[r0:r0 + nr, :] = project(r0, nr).astype(BF16)

    @pl.when((j == 13) | (j == 14))
    def _():
        for r0, nr in row_chunks:
            o_ref[r0:r0 + nr, :] = project(r0, nr).astype(BF16)

    @pl.when(j >= 15)
    def _():
        for r0, nr in row_chunks:
            mg_ref[r0:r0 + nr, :] = project(r0, nr).astype(BF16)


def input_projection(x2d, scale, shift, norm_w, cos_t, sin_t, w_main, w_gates, *, batch, seq):
    n, d = x2d.shape
    tm = min(1024, seq)
    tpb = seq // tm
    nj = w_main.shape[0]
    d_merge = (nj - (9 + 6)) * PROJ_TN
    assert nj == 15 + d_merge // PROJ_TN and seq % tm == 0 and tm % (16 * 16) == 0

    def att_spec(g, dil):
        return pl.BlockSpec((1, dil, tm // dil, PROJ_TN),
                            lambda i, j: (i // tpb, 0, i % tpb, jnp.clip(j - 3 * g, 0, 2)))

    def col_spec(first, count):
        return pl.BlockSpec((tm, PROJ_TN), lambda i, j: (i, jnp.clip(j - first, 0, count - 1)))

    out_shape = [jax.ShapeDtypeStruct((batch, dil, seq // dil, 3 * ATT_GROUP_W), BF16) for dil in ATT_DILATIONS]
    out_shape += [jax.ShapeDtypeStruct((n, 2 * ML_QK_W), F32),
                  jax.ShapeDtypeStruct((n, ML_V_W), BF16),
                  jax.ShapeDtypeStruct((n, ML_V_W), BF16),
                  jax.ShapeDtypeStruct((n, d_merge), BF16),
                  jax.ShapeDtypeStruct((n, GATE_PAD), F32)]
    out_specs = [att_spec(g, dil) for g, dil in enumerate(ATT_DILATIONS)]
    out_specs += [col_spec(9, 2), col_spec(11, 2), col_spec(13, 2), col_spec(15, d_merge // PROJ_TN),
                  pl.BlockSpec((tm, GATE_PAD), lambda i, j: (i, 0))]
    return pl.pallas_call(
        functools.partial(_inproj_kernel, tm=tm),
        out_shape=out_shape,
        grid=(n // tm, nj),
        in_specs=[pl.BlockSpec((tm, d), lambda i, j: (i, 0), pipeline_mode=pl.Buffered(1)),
                  pl.BlockSpec((1, 1, d), lambda i, j: (i // tpb, 0, 0)),
                  pl.BlockSpec((1, 1, d), lambda i, j: (i // tpb, 0, 0)),
                  pl.BlockSpec((1, d), lambda i, j: (0, 0)),
                  pl.BlockSpec((tm, HEAD_DIM), lambda i, j: (i % tpb, 0)),
                  pl.BlockSpec((tm, HEAD_DIM), lambda i, j: (i % tpb, 0)),
                  pl.BlockSpec((1, d, PROJ_TN), lambda i, j: (j, 0, 0)),
                  pl.BlockSpec((d, GATE_PAD), lambda i, j: (0, 0))],
        out_specs=out_specs,
        scratch_shapes=[pltpu.VMEM((tm, d), BF16), pltpu.VMEM((ATT_HEADS_PER_GROUP, tm, HEAD_DIM), F32)],
        compiler_params=_cparams(("parallel", "arbitrary")),
    )(x2d, scale, shift, norm_w, cos_t, sin_t, w_main, w_gates)


def _attention_kernel(*refs, pos_tile):
    groups = [refs[7 * g:7 * g + 7] for g in range(3)]
    o_ref = refs[21]
    kcat, vcat, acc_n, m_n, l_n = refs[22:27]
    i = pl.program_id(2)
    last = pl.num_programs(2) - 1
    sub = ML_CHUNK
    row = lax.broadcasted_iota(I32, (sub, 2 * sub), 0)
    col = lax.broadcasted_iota(I32, (sub, 2 * sub), 1)
    band = (col >= row) & (col - row <= 2 * ATT_HALF)
    for g, dil in enumerate(ATT_DILATIONS):
        q_ref, kp_ref, kc_ref, kn_ref, vp_ref, vc_ref, vn_ref = groups[g]
        r = pos_tile // dil
        for rho in range(dil):
            kcat[0:ATT_HALF] = kp_ref[0, rho]
            kcat[ATT_HALF:ATT_HALF + r] = kc_ref[0, rho]
            kcat[ATT_HALF + r:2 * ATT_HALF + r] = kn_ref[0, rho]
            vcat[0:ATT_HALF] = vp_ref[0, rho]
            vcat[ATT_HALF:ATT_HALF + r] = vc_ref[0, rho]
            vcat[ATT_HALF + r:2 * ATT_HALF + r] = vn_ref[0, rho]
            for s in range(r // sub):
                q = q_ref[0, rho, s * sub:(s + 1) * sub, :]
                k = kcat[s * sub:(s + 2) * sub, :]
                v = vcat[s * sub:(s + 2) * sub, :]
                sc = lax.dot_general(q, k, (((1,), (1,)), ((), ())), preferred_element_type=F32)
                kidx = col + s * sub
                ok = band & ((kidx >= ATT_HALF) | (i > 0)) & ((kidx < r + ATT_HALF) | (i < last))
                sc = jnp.where(ok, sc, MASK_VALUE)
                m = jnp.max(sc, axis=-1, keepdims=True)
                p = jnp.exp(sc - m)
                l = jnp.sum(p, axis=-1, keepdims=True)
                acc = jnp.dot(p.astype(BF16), v, preferred_element_type=F32)
                start = rho + dil * sub * s
                if dil == 1:
                    dst = pl.ds(start, sub)
                else:
                    dst = pl.ds(start, sub, stride=dil)
                acc_n[g, dst, :] = acc
                m_n[g, dst, :] = jnp.broadcast_to(m, (sub, HEAD_DIM))
                l_n[g, dst, :] = jnp.broadcast_to(l, (sub, HEAD_DIM))
    m_all = jnp.maximum(jnp.maximum(m_n[0], m_n[1]), m_n[2])
    num = jnp.zeros_like(m_all)
    den = jnp.zeros_like(m_all)
    for g in range(3):
        w = jnp.exp(m_n[g] - m_all)
        num = num + w * acc_n[g]
        den = den + w * l_n[g]
    o_ref[...] = (num / den).astype(BF16)


def attention_branch(att_groups, *, batch, seq):
    p = min(ATT_POS_TILE, seq)
    assert seq % p == 0 and p % (ML_CHUNK * ATT_DILATIONS[-1]) == 0
    nblk = seq // p
    hb = ATT_HALF
    in_specs = []
    args = []
    for g, dil in enumerate(ATT_DILATIONS):
        r = p // dil
        s_len = seq // dil
        n_half_blocks = s_len // hb

        def cur(col0, r=r, dil=dil):
            return pl.BlockSpec((1, dil, r, HEAD_DIM), lambda b, h, i: (b, 0, i, col0 + h))

        def prev(col0, r=r, dil=dil):
            return pl.BlockSpec((1, dil, hb, HEAD_DIM),
                                lambda b, h, i: (b, 0, jnp.maximum(i * (r // hb) - 1, 0), col0 + h))

        def nxt(col0, r=r, dil=dil, nhb=n_half_blocks):
            return pl.BlockSpec((1, dil, hb, HEAD_DIM),
                                lambda b, h, i: (b, 0, jnp.minimum((i + 1) * (r // hb), nhb - 1), col0 + h))

        kc0, vc0 = ATT_HEADS_PER_GROUP, 2 * ATT_HEADS_PER_GROUP
        in_specs += [cur(0), prev(kc0), cur(kc0), nxt(kc0), prev(vc0), cur(vc0), nxt(vc0)]
        args += [att_groups[g]] * 7
    return pl.pallas_call(
        functools.partial(_attention_kernel, pos_tile=p),
        out_shape=jax.ShapeDtypeStruct((batch * seq, ATT_GROUP_W), BF16),
        grid=(batch, ATT_HEADS_PER_GROUP, nblk),
        in_specs=in_specs,
        out_specs=pl.BlockSpec((p, HEAD_DIM), lambda b, h, i: (b * nblk + i, h)),
        scratch_shapes=[pltpu.VMEM((p + 2 * hb, HEAD_DIM), BF16),
                        pltpu.VMEM((p + 2 * hb, HEAD_DIM), BF16),
                        pltpu.VMEM((3, p, HEAD_DIM), F32),
                        pltpu.VMEM((3, p, HEAD_DIM), F32),
                        pltpu.VMEM((3, p, HEAD_DIM), F32)],
        compiler_params=_cparams(("parallel", "parallel", "arbitrary")),
    )(*args)


def _conv_kernel(prev_ref, cur_ref, next_ref, w_ref, b_ref, o_ref, xcat, *, tt, tpb):
    i = pl.program_id(0)
    first = (i % tpb) == 0
    last = (i % tpb) == tpb - 1
    xcat[0:8] = jnp.where(first, 0.0, prev_ref[...])
    xcat[8:8 + tt] = cur_ref[...]
    xcat[8 + tt:16 + tt] = jnp.where(last, 0.0, next_ref[...])
    acc = jnp.broadcast_to(b_ref[...], (tt, b_ref.shape[1]))
    for j in range(ML_CONV_TAPS):
        acc = acc + w_ref[j:j + 1, :] * xcat[pl.ds(8 - ML_CONV_TAPS // 2 + j, tt), :]
    y = _silu(acc)
    lane = lax.broadcasted_iota(I32, y.shape, 1)
    o_ref[...] = jnp.where(lane < ML_QK_W, y * QK_SCALE, y).astype(BF16)


def mlstm_conv(qk_raw, conv_w, conv_b, *, seq):
    n, c = qk_raw.shape
    tt = min(512, seq)
    tpb = seq // tt
    w_pad = jnp.zeros((8, c), F32).at[:ML_CONV_TAPS].set(conv_w)
    return pl.pallas_call(
        functools.partial(_conv_kernel, tt=tt, tpb=tpb),
        out_shape=jax.ShapeDtypeStruct((n, c), BF16),
        grid=(n // tt,),
        in_specs=[pl.BlockSpec((8, c), lambda i: (jnp.maximum(i * (tt // 8) - 1, 0), 0)),
                  pl.BlockSpec((tt, c), lambda i: (i, 0)),
                  pl.BlockSpec((8, c), lambda i: (jnp.minimum((i + 1) * (tt // 8), n // 8 - 1), 0)),
                  pl.BlockSpec((8, c), lambda i: (0, 0)),
                  pl.BlockSpec((1, c), lambda i: (0, 0))],
        out_specs=pl.BlockSpec((tt, c), lambda i: (i, 0)),
        scratch_shapes=[pltpu.VMEM((tt + 16, c), F32)],
        compiler_params=_cparams(("parallel",)),
    )(qk_raw, qk_raw, qk_raw, w_pad, conv_b.reshape(1, c))


def _log_sigmoid(x):
    return jnp.minimum(x, 0.0) - jnp.log(1.0 + jnp.exp(-jnp.abs(x)))


def _mlstm_direction(q_ref, k_ref, v_ref, g_ref, bias_ref, h_ref, c_scr, n_scr, m_scr, *, reverse):
    L = ML_CHUNK
    row = lax.broadcasted_iota(I32, (L, L), 0)
    col = lax.broadcasted_iota(I32, (L, L), 1)
    causal = (col >= row) if reverse else (col <= row)
    gates = g_ref[...] + bias_ref[...]
    logf = _log_sigmoid(gates)
    b_cols = jnp.dot(causal.astype(F32), logf, preferred_element_type=F32,
                     precision=lax.Precision.HIGHEST)
    gates_t = gates.T
    b_rows = b_cols.T
    edge = 0 if reverse else L - 1
    off = 2 * ML_HEADS if reverse else 0
    for h in range(ML_HEADS):
        st = (ML_HEADS if reverse else 0) + h
        ci = off + h
        cf = off + ML_HEADS + h
        q = q_ref[:, h * HEAD_DIM:(h + 1) * HEAD_DIM]
        k = k_ref[:, h * HEAD_DIM:(h + 1) * HEAD_DIM]
        v = v_ref[:, h * ML_V_DIM:(h + 1) * ML_V_DIM]
        ib_r = gates_t[ci:ci + 1, :]
        ib_c = gates[:, ci:ci + 1]
        b_r = b_rows[cf:cf + 1, :]
        b_c = b_cols[:, cf:cf + 1]
        b_last = b_cols[edge:edge + 1, cf:cf + 1]
        m_prev = m_scr[st][:, 0:1]
        c_prev = c_scr[st]
        n_prev = n_scr[st]
        dlog = jnp.where(causal, b_c - b_r + ib_r, MASK_VALUE)
        inter = b_c + m_prev
        m_i = jnp.maximum(inter, jnp.max(dlog, axis=-1, keepdims=True))
        w = jnp.exp(dlog - m_i)
        s = lax.dot_general(q, k, (((1,), (1,)), ((), ())), preferred_element_type=F32) * w
        sc = jnp.exp(inter - m_i)
        num = jnp.dot(s.astype(BF16), v, preferred_element_type=F32)
        num = num + sc * jnp.dot(q, c_prev.astype(BF16), preferred_element_type=F32)
        qn = jnp.sum(q.astype(F32) * n_prev, axis=-1, keepdims=True)
        den = jnp.sum(s, axis=-1, keepdims=True) + sc * qn
        h_ref[:, h * ML_V_DIM:(h + 1) * ML_V_DIM] = num / jnp.maximum(jnp.abs(den), jnp.exp(-m_i))
        g_r = b_last - b_r + ib_r
        g_c = b_last - b_c + ib_c
        m_new = jnp.maximum(b_last + m_prev, jnp.max(g_r, axis=-1, keepdims=True))
        decay = jnp.exp(b_last + m_prev - m_new)
        kw = k.astype(F32) * jnp.exp(g_c - m_new)
        c_scr[st] = decay * c_prev + lax.dot_general(kw.astype(BF16), v, (((0,), (0,)), ((), ())),
                                                     preferred_element_type=F32)
        n_scr[st] = decay * n_prev + jnp.sum(kw, axis=0, keepdims=True)
        m_scr[st] = jnp.broadcast_to(m_new, (1, LANES))


def _mlstm_kernel(qf_ref, kf_ref, vf_ref, gf_ref, qb_ref, kb_ref, vb_ref, gb_ref, bias_ref,
                  hf_ref, hb_ref, c_scr, n_scr, m_scr):
    @pl.when(pl.program_id(1) == 0)
    def _():
        c_scr[...] = jnp.zeros_like(c_scr)
        n_scr[...] = jnp.zeros_like(n_scr)
        m_scr[...] = jnp.zeros_like(m_scr)

    _mlstm_direction(qf_ref, kf_ref, vf_ref, gf_ref, bias_ref, hf_ref, c_scr, n_scr, m_scr, reverse=False)
    _mlstm_direction(qb_ref, kb_ref, vb_ref, gb_ref, bias_ref, hb_ref, c_scr, n_scr, m_scr, reverse=True)


def mlstm_scan(qk, v, gates, bias_pad, *, batch, seq):
    n = batch * seq
    nc = seq // ML_CHUNK
    fwd = lambda col: (lambda b, c: (b * nc + c, col))
    bwd = lambda col: (lambda b, c: (b * nc + nc - 1 - c, col))

    def specs(at):
        return [pl.BlockSpec((ML_CHUNK, ML_QK_W), at(0)),
                pl.BlockSpec((ML_CHUNK, ML_QK_W), at(1)),
                pl.BlockSpec((ML_CHUNK, ML_V_W), at(0)),
                pl.BlockSpec((ML_CHUNK, GATE_PAD), at(0))]

    out = jax.ShapeDtypeStruct((n, ML_V_W), F32)
    return pl.pallas_call(
        _mlstm_kernel,
        out_shape=[out, out],
        grid=(batch, nc),
        in_specs=specs(fwd) + specs(bwd) + [pl.BlockSpec((1, GATE_PAD), lambda b, c: (0, 0))],
        out_specs=[pl.BlockSpec((ML_CHUNK, ML_V_W), fwd(0)), pl.BlockSpec((ML_CHUNK, ML_V_W), bwd(0))],
        scratch_shapes=[pltpu.VMEM((2 * ML_HEADS, HEAD_DIM, ML_V_DIM), F32),
                        pltpu.VMEM((2 * ML_HEADS, 1, HEAD_DIM), F32),
                        pltpu.VMEM((2 * ML_HEADS, 1, LANES), F32)],
        compiler_params=_cparams(("parallel", "arbitrary")),
    )(qk, qk, v, gates, qk, qk, v, gates, bias_pad)


def _postmix_kernel(att_ref, hf_ref, hb_ref, mlo_ref, mg_ref, x_ref, g1_ref, sc2_ref, sh2_ref,
                    n2w_ref, mlnw_ref, wml_ref, watt_ref, wo_ref, wrh_ref, wrl_ref,
                    x1_ref, h2_ref, lg_ref, *, d):
    hm = hf_ref[...] + hb_ref[...]
    parts = []
    for h in range(ML_HEADS):
        hh = hm[:, h * ML_V_DIM:(h + 1) * ML_V_DIM]
        parts.append(hh * lax.rsqrt(jnp.mean(hh * hh, axis=-1, keepdims=True) + NORM_EPS))
    hn = jnp.concatenate(parts, axis=1) * mlnw_ref[...]
    ml_act = (_sigmoid(mlo_ref[...].astype(F32)) * hn).astype(BF16)
    y_ml = jnp.dot(ml_act, wml_ref[...], preferred_element_type=F32)
    y_att = jnp.dot(att_ref[...], watt_ref[...], preferred_element_type=F32)
    g_ml = _sigmoid(mg_ref[:, 0:d].astype(F32))
    g_att = _sigmoid(mg_ref[:, d:2 * d].astype(F32))
    merged = (g_ml * y_ml + g_att * y_att).astype(BF16)
    out = jnp.dot(merged, wo_ref[...], preferred_element_type=F32)
    x1 = x_ref[...] + g1_ref[0] * out
    x1_ref[...] = x1
    ms = jnp.mean(x1 * x1, axis=-1, keepdims=True)
    h2 = x1 * lax.rsqrt(ms + NORM_EPS) * n2w_ref[...]
    h2 = h2 * (1.0 + sc2_ref[0]) + sh2_ref[0]
    h2_ref[...] = _pack_pair(h2[:, :d // 2], h2[:, d // 2:])
    h_hi = h2.astype(BF16)
    h_lo = (h2 - h_hi.astype(F32)).astype(BF16)
    lg_ref[...] = (jnp.dot(h_hi, wrh_ref[...], preferred_element_type=F32)
                   + jnp.dot(h_lo, wrh_ref[...], preferred_element_type=F32)
                   + jnp.dot(h_hi, wrl_ref[...], preferred_element_type=F32))


def post_mixer(att_o, h_f, h_b, ml_o, merge, x2d, gate1, scale2, shift2, norm2_w, ml_norm_w,
               w_ml_out, w_att_out, w_o, w_router_hi, w_router_lo, *, seq):
    n, d = x2d.shape
    tm = 256
    tpb = seq // tm
    const = lambda shape: pl.BlockSpec(shape, lambda i: tuple(0 for _ in shape), pipeline_mode=pl.Buffered(1))
    rows = lambda w: pl.BlockSpec((tm, w), lambda i: (i, 0))
    per_b = pl.BlockSpec((1, 1, d), lambda i: (i // tpb, 0, 0))
    return pl.pallas_call(
        functools.partial(_postmix_kernel, d=d),
        out_shape=[jax.ShapeDtypeStruct((n, d), F32),
                   jax.ShapeDtypeStruct((n, d // 2), I32),
                   jax.ShapeDtypeStruct((n, LANES), F32)],
        grid=(n // tm,),
        in_specs=[rows(ATT_GROUP_W), rows(ML_V_W), rows(ML_V_W), rows(ML_V_W), rows(2 * d), rows(d),
                  per_b, per_b, per_b, const((1, d)), const((1, ML_V_W)),
                  const((ML_V_W, d)), const((ATT_GROUP_W, d)), const((d, d)), const((d, LANES)), const((d, LANES))],
        out_specs=[rows(d), rows(d // 2), rows(LANES)],
        compiler_params=_cparams(("parallel",)),
    )(att_o, h_f, h_b, ml_o, merge, x2d, gate1, scale2, shift2, norm2_w, ml_norm_w,
      w_ml_out, w_att_out, w_o, w_router_hi, w_router_lo)


def _split3(x):
    p0 = x.astype(BF16)
    r1 = x - p0.astype(F32)
    p1 = r1.astype(BF16)
    p2 = (r1 - p1.astype(F32)).astype(BF16)
    return p0, p1, p2


def _route_select_kernel(lg_ref, gate_ref, dest_ref, lcs_ref, bp_ref, bs_ref, off_ref, cnt_ref, *, cap, n_tok):
    lg = lg_ref[...]
    e, nb, _ = lg.shape
    mx = jnp.max(lg, axis=0, keepdims=True)
    ex = jnp.exp(lg - mx)
    aff = ex / jnp.sum(ex, axis=0, keepdims=True)
    bits = lax.bitcast_convert_type(aff, I32)

    def count(pred):
        c = jnp.sum(jnp.where(pred, 1.0, 0.0), axis=1, keepdims=True)
        return jnp.sum(c, axis=2, keepdims=True)

    capf = jnp.float32(cap)

    def thr_step(_, c):
        lo, hi = c
        mid = lo + lax.shift_right_logical(hi - lo + 1, 1)
        ok = count(bits >= mid) >= capf
        return jnp.where(ok, mid, lo), jnp.where(ok, hi, mid - 1)

    lo0 = jnp.zeros((e, 1, 1), I32)
    hi0 = jnp.full((e, 1, 1), 0x7F800000, I32)
    thr, _ = lax.fori_loop(0, 31, thr_step, (lo0, hi0))
    gt = bits > thr
    eq = bits == thr
    need = capf - count(gt)
    nidx = (lax.broadcasted_iota(I32, (1, nb, LANES), 1) * LANES + lax.broadcasted_iota(I32, (1, nb, LANES), 2))

    def tie_step(_, c):
        lo, hi = c
        mid = lax.shift_right_logical(lo + hi, 1)
        ok = count(eq & (nidx < mid)) >= need
        return jnp.where(ok, lo, mid), jnp.where(ok, mid, hi)

    n_iter = max(1, (n_tok - 1).bit_length() + 1)
    _, jcut = lax.fori_loop(0, n_iter, tie_step, (jnp.zeros((e, 1, 1), I32), jnp.full((e, 1, 1), n_tok, I32)))
    mask = jnp.where(gt | (eq & (nidx < jcut)), 1.0, 0.0)
    gate_ref[...] = aff * mask

    jr = lax.broadcasted_iota(I32, (LANES, LANES), 0)
    jc = lax.broadcasted_iota(I32, (LANES, LANES), 1)
    tri_incl = jnp.where(jr <= jc, 1.0, 0.0).astype(BF16)
    br = lax.broadcasted_iota(I32, (nb, nb), 0)
    bc = lax.broadcasted_iota(I32, (nb, nb), 1)
    tri_prev = jnp.where(bc < br, 1.0, 0.0).astype(BF16)

    lcs = jnp.dot(mask.reshape(e * nb, LANES).astype(BF16), tri_incl, preferred_element_type=F32)
    lcs = lcs.reshape(e, nb, LANES)
    lcs_ref[...] = lcs.astype(BF16)
    cnt = jnp.zeros((nb, LANES), F32)
    ranks = []
    for ei in range(e):
        bs = jnp.broadcast_to(lcs[ei][:, LANES - 1:LANES], (nb, LANES))
        bs_ref[ei] = bs
        bp_ref[ei] = jnp.dot(tri_prev, bs.astype(BF16), preferred_element_type=F32)
        ranks.append(cnt)
        cnt = cnt + mask[ei]
    lcs_k = jnp.dot(cnt.astype(BF16), tri_incl, preferred_element_type=F32)
    bs_k = jnp.broadcast_to(lcs_k[:, LANES - 1:LANES], (nb, LANES))
    bs_hi = jnp.floor(bs_k * (1.0 / 64.0))
    bs_lo = bs_k - 64.0 * bs_hi
    bp_k = (64.0 * jnp.dot(tri_prev, bs_hi.astype(BF16), preferred_element_type=F32)
            + jnp.dot(tri_prev, bs_lo.astype(BF16), preferred_element_type=F32))
    off = bp_k + lcs_k - cnt
    off_ref[...] = off.astype(I32)
    cnt_ref[...] = cnt.astype(I32)
    for ei in range(e):
        dest_ref[ei] = jnp.where(mask[ei] > 0.0, off + ranks[ei], -1.0)


def _route_compact_kernel(lcs_ref, bp_ref, bs_ref, dest_ref, gate_ref, idx_out, dst_out, gate_out, *, cap):
    lcs = lcs_ref[0]
    bp = bp_ref[0]
    bs = bs_ref[0]
    nb = lcs.shape[0]
    blk_id = lax.broadcasted_iota(I32, (nb, LANES), 0).astype(F32)
    lane = lax.broadcasted_iota(I32, (1, LANES), 1).astype(F32)
    oh_parts, base_parts, blk_parts = [], [], []
    for k in range(cap // LANES):
        r = lane + float(k * LANES)
        hit = (bp <= r) & (r < bp + bs)
        oh_parts.append(jnp.where(hit, 1.0, 0.0).astype(BF16))
        base_parts.append(jnp.sum(jnp.where(hit, bp, 0.0), axis=0, keepdims=True))
        blk_parts.append(jnp.sum(jnp.where(hit, blk_id, 0.0), axis=0, keepdims=True))
    onehot = jnp.concatenate(oh_parts, axis=1)
    base = jnp.concatenate(base_parts, axis=1)
    blk = jnp.concatenate(blk_parts, axis=1)
    rows = lax.broadcasted_iota(I32, (1, cap), 1).astype(F32)
    contract0 = (((0,), (0,)), ((), ()))
    counts = lax.dot_general(lcs, onehot, contract0, preferred_element_type=F32)
    jloc = jnp.sum(jnp.where(counts <= rows - base, 1.0, 0.0), axis=0, keepdims=True)
    dvals = jnp.zeros((LANES, cap), F32)
    for piece in _split3(dest_ref[0]):
        dvals = dvals + lax.dot_general(piece, onehot, contract0, preferred_element_type=F32)
    jcol = lax.broadcasted_iota(I32, (LANES, cap), 0).astype(F32)
    dst = jnp.sum(jnp.where(jcol == jloc, dvals, 0.0), axis=0, keepdims=True)
    gvals = jnp.zeros((LANES, cap), F32)
    for piece in _split3(gate_ref[0]):
        gvals = gvals + lax.dot_general(piece, onehot, contract0, preferred_element_type=F32)
    idx_out[0] = (blk * float(LANES) + jloc).astype(I32)
    dst_out[0] = dst.astype(I32)
    gate_out[0] = jnp.sum(jnp.where(jcol == jloc, gvals, 0.0), axis=0, keepdims=True)


def route_tokens(logits, n_experts):
    n = logits.shape[0]
    e = n_experts
    nb = n // LANES
    cap = CAPACITY_FACTOR * n // e
    assert n % LANES == 0 and cap % LANES == 0
    lg = logits[:, :e].T.reshape(e, nb, LANES)
    big = lambda dt: jax.ShapeDtypeStruct((e, nb, LANES), dt)
    small = jax.ShapeDtypeStruct((nb, LANES), I32)
    gate, dest, lcs, bp, bs, off, cnt = pl.pallas_call(
        functools.partial(_route_select_kernel, cap=cap, n_tok=n),
        out_shape=[big(F32), big(F32), big(BF16), big(F32), big(F32), small, small],
        compiler_params=pltpu.CompilerParams(vmem_limit_bytes=VMEM_LIMIT),
    )(lg)
    per_e = lambda: pl.BlockSpec((1, nb, LANES), lambda i: (i, 0, 0))
    slot = lambda: pl.BlockSpec((1, 1, cap), lambda i: (i, 0, 0))
    idx, dst, gate_slot = pl.pallas_call(
        functools.partial(_route_compact_kernel, cap=cap),
        out_shape=[jax.ShapeDtypeStruct((e, 1, cap), I32), jax.ShapeDtypeStruct((e, 1, cap), I32),
                   jax.ShapeDtypeStruct((e, 1, cap), F32)],
        grid=(e,),
        in_specs=[per_e(), per_e(), per_e(), per_e(), per_e()],
        out_specs=[slot(), slot(), slot()],
        compiler_params=_cparams(("parallel",)),
    )(lcs, bp, bs, dest, gate)
    return idx.reshape(-1), dst.reshape(-1), gate_slot.reshape(-1), off.reshape(-1), cnt.reshape(-1)


def _expert_kernel(idx_ref, idxn_ref, dst_ref, gate_ref, wg_ref, wu_ref, wd_ref, h2_hbm, z_hbm,
                   xbuf, zbuf, xs, acc, gsem, ssem, *, d, ff, tr):
    s = pl.program_id(0)
    nsteps = pl.num_programs(0)
    half = d // 2

    def start_gather(ids, rows=range(tr)):
        for r in rows:
            pltpu.make_async_copy(h2_hbm.at[pl.ds(ids[0, 0, r], 1), :], xbuf.at[pl.ds(r, 1), :], gsem.at[0]).start()

    def wait_gather():
        pltpu.make_async_copy(h2_hbm.at[pl.ds(0, tr), :], xbuf, gsem.at[0]).wait()

    def wait_scatter():
        pltpu.make_async_copy(zbuf, z_hbm.at[pl.ds(0, tr), :], ssem.at[0]).wait()

    @pl.when(s == 0)
    def _():
        start_gather(idx_ref)

    wait_gather()
    lo, hi = _unpack_pair(xbuf[...])
    xs[:, :half] = lo.astype(BF16)
    xs[:, half:] = hi.astype(BF16)
    x = xs[...]
    tf = min(ff, 512)
    n_groups = 3 * (ff // tf)
    per_group = -(-tr // n_groups)
    groups = [range(i * per_group, min((i + 1) * per_group, tr)) for i in range(n_groups)]
    for f in range(ff // tf):
        g = jnp.dot(x, wg_ref[0, :, f * tf:(f + 1) * tf], preferred_element_type=F32)
        start_gather(idxn_ref, groups[3 * f])
        u = jnp.dot(x, wu_ref[0, :, f * tf:(f + 1) * tf], preferred_element_type=F32)
        start_gather(idxn_ref, groups[3 * f + 1])
        hid = (_silu(g) * u).astype(BF16)
        part = jnp.dot(hid, wd_ref[0, f * tf:(f + 1) * tf, :], preferred_element_type=F32)
        start_gather(idxn_ref, groups[3 * f + 2])
        if f == 0:
            acc[...] = part
        else:
            acc[...] += part

    @pl.when(s > 0)
    def _():
        wait_scatter()

    gcol = jnp.broadcast_to(gate_ref[0], (LANES, tr)).T
    gwide = jnp.concatenate([gcol] * (half // LANES), axis=1)
    zbuf[...] = _pack_pair(acc[:, :half] * gwide, acc[:, half:] * gwide)
    for r in range(tr):
        pltpu.make_async_copy(zbuf.at[pl.ds(r, 1), :], z_hbm.at[pl.ds(dst_ref[0, 0, r], 1), :], ssem.at[0]).start()

    @pl.when(s == nsteps - 1)
    def _():
        wait_scatter()
        wait_gather()


def expert_ffn(h2_packed, idx, dest, gate, w_gate, w_up, w_down):
    n, half = h2_packed.shape
    e, d, ff = w_gate.shape
    total = idx.shape[0]
    cap = total // e
    tr = min(EXPERT_ROWS, cap)
    nsteps = total // tr
    per_e = cap // tr
    idx3 = idx.reshape(nsteps, 1, tr)
    dst3 = dest.reshape(nsteps, 1, tr)
    gate3 = gate.reshape(nsteps, 1, tr)
    smem = lambda f: pl.BlockSpec((1, 1, tr), f, memory_space=pltpu.SMEM)
    wspec = lambda shape, nbuf=1: pl.BlockSpec(shape, lambda s: (s // per_e, 0, 0), pipeline_mode=pl.Buffered(nbuf))
    return pl.pallas_call(
        functools.partial(_expert_kernel, d=d, ff=ff, tr=tr),
        out_shape=jax.ShapeDtypeStruct((total, half), I32),
        grid=(nsteps,),
        in_specs=[smem(lambda s: (s, 0, 0)),
                  smem(lambda s: (jnp.minimum(s + 1, nsteps - 1), 0, 0)),
                  smem(lambda s: (s, 0, 0)),
                  pl.BlockSpec((1, 1, tr), lambda s: (s, 0, 0)),
                  wspec((1, d, ff), 2), wspec((1, d, ff)), wspec((1, ff, d)),
                  pl.BlockSpec(memory_space=pl.ANY)],
        out_specs=pl.BlockSpec(memory_space=pl.ANY),
        scratch_shapes=[pltpu.VMEM((tr, half), I32),
                        pltpu.VMEM((tr, half), I32),
                        pltpu.VMEM((tr, d), BF16),
                        pltpu.VMEM((tr, d), F32),
                        pltpu.SemaphoreType.DMA((1,)),
                        pltpu.SemaphoreType.DMA((1,))],
        compiler_params=_cparams(("arbitrary",)),
    )(idx3, idx3, dst3, gate3, w_gate, w_up, w_down, h2_packed)


def _combine_kernel(tile_ref, chunk_ref, flag_ref, z_ref, oc_ref, x_ref, g2_ref, fw_ref, o_ref,
                    acc, *, d, tm, tr, final):
    w = pl.program_id(0)
    flags = flag_ref[w]

    @pl.when((flags & 2) != 0)
    def _():
        acc[...] = jnp.zeros_like(acc)

    @pl.when((flags & 1) != 0)
    def _():
        lo, hi = _unpack_pair(z_ref[...])
        zs = jnp.concatenate([lo.astype(BF16), hi.astype(BF16)], axis=1)
        zrow = lax.broadcasted_iota(I32, (tm, tr), 1) + chunk_ref[w] * tr
        oc = oc_ref[...]
        off = oc[:, 0:1]
        cnt = oc[:, 1:2]
        sel = jnp.where((zrow >= off) & (zrow < off + cnt), 1.0, 0.0).astype(BF16)
        acc[...] += jnp.dot(sel, zs, preferred_element_type=F32)

    @pl.when((flags & 4) != 0)
    def _():
        x2 = x_ref[...] + g2_ref[0] * acc[...]
        if final:
            ms = jnp.mean(x2 * x2, axis=-1, keepdims=True)
            x2 = x2 * lax.rsqrt(ms + NORM_EPS) * fw_ref[...]
        o_ref[...] = x2


def moe_combine(z_sorted, item_tile, item_chunk, item_flags, off_cnt, x1, gate2, final_w, *, seq, final):
    n, d = x1.shape
    total, half = z_sorted.shape
    tm = min(MOE_ROWS, n)
    tr = min(MOE_ROWS, total)
    tpb = seq // tm
    n_items = item_tile.shape[0]
    return pl.pallas_call(
        functools.partial(_combine_kernel, d=d, tm=tm, tr=tr, final=final),
        out_shape=jax.ShapeDtypeStruct((n, d), F32),
        grid_spec=pltpu.PrefetchScalarGridSpec(
            num_scalar_prefetch=3,
            grid=(n_items,),
            in_specs=[pl.BlockSpec((tr, half), lambda w, t, c, f: (c[w], 0)),
                      pl.BlockSpec((tm, 2), lambda w, t, c, f: (t[w], 0)),
                      pl.BlockSpec((tm, d), lambda w, t, c, f: (t[w], 0)),
                      pl.BlockSpec((1, 1, d), lambda w, t, c, f: (t[w] // tpb, 0, 0)),
                      pl.BlockSpec((1, d), lambda w, t, c, f: (0, 0))],
            out_specs=pl.BlockSpec((tm, d), lambda w, t, c, f: (t[w], 0)),
            scratch_shapes=[pltpu.VMEM((tm, d), F32)]),
        compiler_params=_cparams(("arbitrary",)),
    )(item_tile, item_chunk, item_flags, z_sorted, off_cnt, x1, gate2, final_w)


def _combine_items(off, n, total, tm, tr):
    nt = n // tm
    nchunks = total // tr
    first = off.reshape(nt, tm)[:, 0]
    end = jnp.concatenate([first[1:], jnp.array([total], I32)])
    c_lo = jnp.minimum(first // tr, nchunks - 1)
    c_hi = jnp.where(end > first, (end - 1) // tr, c_lo)
    cnt = c_hi - c_lo + 1
    start = jnp.cumsum(cnt) - cnt
    n_items = nt + nchunks
    w = jnp.arange(n_items, dtype=I32)
    t = jnp.clip(jnp.searchsorted(start, w, side="right").astype(I32) - 1, 0, nt - 1)
    k = w - start[t]
    valid = k < cnt[t]
    k = jnp.minimum(k, cnt[t] - 1)
    flags = jnp.where(valid, 1 + 2 * (k == 0) + 4 * (k == cnt[t] - 1), 0)
    return t.astype(I32), (c_lo[t] + k).astype(I32), flags.astype(I32)


def _rope_tables(seq):
    freqs = jnp.float32(ROT_THETA) ** (-jnp.arange(ROT_HALF, dtype=F32) / ROT_HALF)
    ang = jnp.arange(seq, dtype=F32)[:, None] * freqs[None, :]
    cos = jnp.cos(ang)
    sin = jnp.sin(ang)
    ones = jnp.ones((seq, HEAD_DIM - 2 * ROT_HALF), F32)
    cos_t = jnp.concatenate([cos, cos, ones], axis=1)
    sin_t = jnp.concatenate([-sin, sin, 0.0 * ones], axis=1)
    return cos_t, sin_t


def _cast_kernel(x_ref, o_ref):
    o_ref[0] = x_ref[0, 0].astype(BF16)


def cast_layer_bf16(w, l):
    _, e, a, b = w.shape
    ta = min(a, 1024)
    return pl.pallas_call(
        _cast_kernel,
        out_shape=jax.ShapeDtypeStruct((e, a, b), BF16),
        grid=(e, a // ta),
        in_specs=[pl.BlockSpec((1, 1, ta, b), lambda i, j: (l, i, j, 0))],
        out_specs=pl.BlockSpec((1, ta, b), lambda i, j: (i, j, 0)),
        compiler_params=_cparams(("parallel", "parallel")),
    )(w)


def _prep_layer(p, l):
    w_in = p["w_in"][l]
    aw = 3 * ATT_GROUP_W
    q, k, v = w_in[:, 0:aw], w_in[:, aw:2 * aw], w_in[:, 2 * aw:3 * aw]
    cols = []
    for g in range(3):
        sl = slice(g * ATT_GROUP_W, (g + 1) * ATT_GROUP_W)
        cols += [q[:, sl], k[:, sl], v[:, sl]]
    o = 3 * aw
    cols.append(w_in[:, o:o + 2 * ML_QK_W + 2 * ML_V_W])
    o += 2 * ML_QK_W + 2 * ML_V_W
    w_gates = w_in[:, o:o + 4 * ML_HEADS]
    o += 4 * ML_HEADS
    cols.append(w_in[:, o:])
    d = w_in.shape[0]
    w_router = jnp.zeros((d, LANES), F32).at[:, :N_EXPERTS].set(p["w_router"][l])
    return dict(
        w_main=jnp.stack([t for c in cols for t in jnp.split(c, c.shape[1] // PROJ_TN, axis=1)]).astype(BF16),
        w_gates=jnp.zeros((d, GATE_PAD), BF16).at[:, :4 * ML_HEADS].set(w_gates.astype(BF16)),
        bias_pad=jnp.zeros((1, GATE_PAD), F32).at[0, :4 * ML_HEADS].set(p["b_mgate"][l]),
        w_ml_out=p["w_ml_out"][l].astype(BF16),
        w_att_out=p["w_att_out"][l].astype(BF16),
        w_o=p["w_o"][l].astype(BF16),
        w_router_hi=w_router.astype(BF16),
        w_router_lo=(w_router - w_router.astype(BF16).astype(F32)).astype(BF16),
        w_exp_gate=cast_layer_bf16(p["w_exp_gate"], l),
        w_exp_up=cast_layer_bf16(p["w_exp_up"], l),
        w_exp_down=cast_layer_bf16(p["w_exp_down"], l),
    )


def _moe_sublayer(x1, h2p, logits, gate2, lw, final_w, *, seq, final):
    n = x1.shape[0]
    idx, dest, gate, off, cnt = route_tokens(logits, N_EXPERTS)
    z = expert_ffn(h2p, idx, dest, gate, lw["w_exp_gate"], lw["w_exp_up"], lw["w_exp_down"])
    tm = min(MOE_ROWS, n)
    tr = min(MOE_ROWS, z.shape[0])
    items = _combine_items(off, n, z.shape[0], tm, tr)
    return moe_combine(z, *items, jnp.stack([off, cnt], axis=1), x1, gate2, final_w, seq=seq, final=final)


def _run_group(x, mods, layers, p, cos_t, sin_t):
    batch, seq, d = x.shape
    n = batch * seq
    x2d = x.reshape(n, d)
    depth = len(layers)
    for l, lw in enumerate(layers):
        shift1, scale1, gate1, shift2, scale2, gate2 = [m.reshape(batch, 1, d) for m in jnp.split(mods[l], 6, axis=-1)]
        a0, a1, a2, ml_qk, ml_v, ml_o, merge, gates = input_projection(
            x2d, scale1, shift1, p["norm1_w"][l].reshape(1, d), cos_t, sin_t, lw["w_main"], lw["w_gates"],
            batch=batch, seq=seq)
        att_o = attention_branch((a0, a1, a2), batch=batch, seq=seq)
        qk = mlstm_conv(ml_qk, p["conv_w"][l], p["conv_b"][l], seq=seq)
        h_f, h_b = mlstm_scan(qk, ml_v, gates, lw["bias_pad"], batch=batch, seq=seq)
        x1, h2p, logits = post_mixer(att_o, h_f, h_b, ml_o, merge, x2d, gate1, scale2, shift2,
                                     p["norm2_w"][l].reshape(1, d), p["ml_norm_w"][l].reshape(1, ML_V_W),
                                     lw["w_ml_out"], lw["w_att_out"], lw["w_o"], lw["w_router_hi"], lw["w_router_lo"], seq=seq)
        x2d = _moe_sublayer(x1, h2p, logits, gate2, lw, p["final_norm_w"].reshape(1, d),
                            seq=seq, final=(l == depth - 1))
    return x2d.reshape(batch, seq, d)


def kernel(x_prompt, x_sample, c_prompt, c_sample, norm1_w, norm2_w, w_ada, b_ada, w_in, b_mgate, conv_w, conv_b, ml_norm_w, w_ml_out, w_att_out, w_o, w_router, w_exp_gate, w_exp_up, w_exp_down, final_norm_w):
    p = dict(norm1_w=norm1_w, norm2_w=norm2_w, w_in=w_in, b_mgate=b_mgate, conv_w=conv_w, conv_b=conv_b,
             ml_norm_w=ml_norm_w, w_ml_out=w_ml_out, w_att_out=w_att_out, w_o=w_o, w_router=w_router,
             w_exp_gate=w_exp_gate, w_exp_up=w_exp_up, w_exp_down=w_exp_down, final_norm_w=final_norm_w)
    depth = w_in.shape[0]
    bp, bs = c_prompt.shape[0], c_sample.shape[0]
    m_pad = -(-(bp + bs) // 8) * 8
    c_pad = jnp.zeros((m_pad, c_prompt.shape[1]), F32).at[:bp].set(c_prompt).at[bp:bp + bs].set(c_sample)
    mods = ada_modulation(c_pad, w_ada, b_ada)
    layers = [_prep_layer(p, l) for l in range(depth)]
    outs = []
    for x, rows in ((x_prompt, slice(0, bp)), (x_sample, slice(bp, bp + bs))):
        cos_t, sin_t = _rope_tables(x.shape[1])
        outs.append(_run_group(x, mods[:, rows], layers, p, cos_t, sin_t))
    return tuple(outs)
```
